```python
import jax, jax.numpy as jnp
from jax import lax
import numpy as np

D_MODEL = 1024
BATCH = 8
SEQ = 2048
DEPTH = 4

CHUNK = 64
Q_BLOCK = 128
N_MIXERS = 3
SB_HEADS = 16
SB_HEAD_DIM = D_MODEL // SB_HEADS
CONV_WIDTH = 3
ML_HEADS = 8
ML_QK_DIM = D_MODEL // (2 * ML_HEADS)
ML_V_DIM = D_MODEL // ML_HEADS
ML_QK_WIDTH = ML_HEADS * ML_QK_DIM
ML_IN_WIDTH = 2 * ML_QK_WIDTH + 2 * D_MODEL + 2 * ML_HEADS
D_FF = ((8 * D_MODEL + 3 * 256 - 1) // (3 * 256)) * 256
N_SB = (DEPTH + 2) // N_MIXERS
N_GC = (DEPTH + 1) // N_MIXERS
N_ML = DEPTH // N_MIXERS
RMS_EPS = 1e-6

kernel_name = "hybrid_stickbreak_conv_mlstm_encoder"


def rmsnorm(x, g):
    xf = x.astype(jnp.float32)
    y = xf * lax.rsqrt(jnp.mean(xf * xf, axis=-1, keepdims=True) + RMS_EPS)
    return (y * g.astype(jnp.float32)).astype(x.dtype)


def stick_breaking_mixer(h, w_qkv, w_o):
    bsz, seq, _ = h.shape
    q, k, v = jnp.split(h @ w_qkv, 3, axis=-1)

    def heads(t):
        return t.reshape(bsz, seq, SB_HEADS, SB_HEAD_DIM).transpose(0, 2, 1, 3)

    q, k, v = heads(q), heads(k), heads(v)
    scale = SB_HEAD_DIM ** -0.5
    outs = []
    for blk in range(seq // Q_BLOCK):
        t0, t1 = blk * Q_BLOCK, (blk + 1) * Q_BLOCK
        z = jnp.einsum('bhqd,bhkd->bhqk', q[:, :, t0:t1], k[:, :, :t1]).astype(jnp.float32) * scale
        before = jnp.arange(t1)[None, :] < jnp.arange(t0, t1)[:, None]
        log_stay = jnp.where(before, jax.nn.log_sigmoid(-z), 0.0)
        log_pass = lax.cumsum(log_stay, axis=3, reverse=True) - log_stay
        a = jnp.where(before, jnp.exp(jax.nn.log_sigmoid(z) + log_pass), 0.0)
        outs.append(jnp.einsum('bhqk,bhkd->bhqd', a.astype(v.dtype), v[:, :, :t1]))
    o = jnp.concatenate(outs, axis=2).transpose(0, 2, 1, 3).reshape(bsz, seq, D_MODEL)
    return o @ w_o


def gated_conv_mixer(h, w_in, conv_w, conv_b, w_out):
    b_gate, c_gate, u = jnp.split(h @ w_in, 3, axis=-1)
    cu = c_gate * u
    y = lax.conv_general_dilated(
        cu, conv_w[:, None, :].astype(cu.dtype), window_strides=(1,),
        padding=[(CONV_WIDTH - 1, 0)], dimension_numbers=('NWC', 'WIO', 'NWC'),
        feature_group_count=D_MODEL) + conv_b
    return (b_gate * y) @ w_out


def _mlstm_chunk_step(carry, inp):
    c_state, n_state, m_state = carry
    qc, kc, vc, igc, lfc = inp
    L = qc.shape[2]
    b = jnp.cumsum(lfc, axis=-1)
    causal = jnp.tril(jnp.ones((L, L), dtype=bool))
    log_d = jnp.where(causal, b[..., :, None] - b[..., None, :] + igc[..., None, :], -jnp.inf)
    m_inter = b + m_state[..., None]
    m_t = jnp.maximum(m_inter, jnp.max(log_d, axis=-1))
    d_mat = jnp.exp(log_d - m_t[..., None])
    inter = jnp.exp(m_inter - m_t)
    s = jnp.einsum('bhtd,bhsd->bhts', qc, kc) * d_mat
    num = jnp.einsum('bhts,bhsv->bhtv', s, vc) + inter[..., None] * jnp.einsum('bhvd,bhtd->bhtv', c_state, qc)
    den = jnp.sum(s, axis=-1) + inter * jnp.einsum('bhd,bhtd->bht', n_state, qc)
    h_out = num / jnp.maximum(jnp.abs(den), jnp.exp(-m_t))[..., None]
    b_last = b[..., -1]
    log_w = b_last[..., None] - b + igc
    m_new = jnp.maximum(b_last + m_state, jnp.max(log_w, axis=-1))
    w = jnp.exp(log_w - m_new[..., None])
    decay = jnp.exp(b_last + m_state - m_new)
    c_new = decay[..., None, None] * c_state + jnp.einsum('bhs,bhsv,bhsd->bhvd', w, vc, kc)
    n_new = decay[..., None] * n_state + jnp.einsum('bhs,bhsd->bhd', w, kc)
    return (c_new, n_new, m_new), h_out


def mlstm_mixer(h, w_in, b_i, b_f, w_out):
    bsz, seq, _ = h.shape
    cuts = [ML_QK_WIDTH, 2 * ML_QK_WIDTH, 2 * ML_QK_WIDTH + D_MODEL,
            2 * ML_QK_WIDTH + 2 * D_MODEL, 2 * ML_QK_WIDTH + 2 * D_MODEL + ML_HEADS]
    q, k, v, o_pre, i_pre, f_pre = jnp.split(h @ w_in, cuts, axis=-1)

    def heads(t, dh):
        return t.reshape(bsz, seq, ML_HEADS, dh).transpose(0, 2, 1, 3).astype(jnp.float32)

    q = heads(q, ML_QK_DIM)
    k = heads(k, ML_QK_DIM) * (ML_QK_DIM ** -0.5)
    v = heads(v, ML_V_DIM)
    ig = (i_pre + b_i).astype(jnp.float32).transpose(0, 2, 1)
    lf = jax.nn.log_sigmoid((f_pre + b_f).astype(jnp.float32)).transpose(0, 2, 1)
    n_chunks = seq // CHUNK

    def chunks(t):
        return jnp.moveaxis(t.reshape(bsz, ML_HEADS, n_chunks, CHUNK, *t.shape[3:]), 2, 0)

    init = (jnp.zeros((bsz, ML_HEADS, ML_V_DIM, ML_QK_DIM), jnp.float32),
            jnp.zeros((bsz, ML_HEADS, ML_QK_DIM), jnp.float32),
            jnp.zeros((bsz, ML_HEADS), jnp.float32))
    _, hs = lax.scan(_mlstm_chunk_step, init, (chunks(q), chunks(k), chunks(v), chunks(ig), chunks(lf)))
    hs = jnp.moveaxis(hs, 0, 2).reshape(bsz, ML_HEADS, seq, ML_V_DIM)
    hs = hs.transpose(0, 2, 1, 3).reshape(bsz, seq, D_MODEL).astype(h.dtype)
    return (jax.nn.sigmoid(o_pre) * hs) @ w_out


def swiglu_ffn(h, w_gu, w_down):
    g, u = jnp.split(h @ w_gu, 2, axis=-1)
    return (jax.nn.silu(g) * u) @ w_down


def setup_inputs(seed: int = 0) -> dict:
    key = jax.random.key(seed)
    ks = jax.random.split(key, 20)
    f32 = jnp.float32

    def dense(k, shape):
        return jax.random.normal(k, shape, f32) * (shape[-2] ** -0.5)

    def gain(k):
        return 1.0 + 0.05 * jax.random.normal(k, (DEPTH, D_MODEL), f32)

    return {
        "x": jax.random.normal(ks[0], (BATCH, SEQ, D_MODEL), f32),
        "ln_mix_pre": gain(ks[1]),
        "ln_mix_post": gain(ks[2]),
        "ln_ffn_pre": gain(ks[3]),
        "ln_ffn_post": gain(ks[4]),
        "sb_w_qkv": dense(ks[5], (N_SB, D_MODEL, 3 * D_MODEL)),
        "sb_w_o": dense(ks[6], (N_SB, D_MODEL, D_MODEL)),
        "gc_w_in": dense(ks[7], (N_GC, D_MODEL, 3 * D_MODEL)),
        "gc_conv_w": jax.random.normal(ks[8], (N_GC, CONV_WIDTH, D_MODEL), f32) * (CONV_WIDTH ** -0.5),
        "gc_conv_b": 0.02 * jax.random.normal(ks[9], (N_GC, D_MODEL), f32),
        "gc_w_out": dense(ks[10], (N_GC, D_MODEL, D_MODEL)),
        "ml_w_in": dense(ks[11], (N_ML, D_MODEL, ML_IN_WIDTH)),
        "ml_b_i": 0.1 * jax.random.normal(ks[12], (N_ML, ML_HEADS), f32),
        "ml_b_f": 3.0 + 0.5 * jax.random.normal(ks[13], (N_ML, ML_HEADS), f32),
        "ml_w_out": dense(ks[14], (N_ML, D_MODEL, D_MODEL)),
        "ffn_w_gu": dense(ks[15], (DEPTH, D_MODEL, 2 * D_FF)),
        "ffn_w_down": dense(ks[16], (DEPTH, D_FF, D_MODEL)),
    }


def reference(x, ln_mix_pre, ln_mix_post, ln_ffn_pre, ln_ffn_post,
              sb_w_qkv, sb_w_o, gc_w_in, gc_conv_w, gc_conv_b, gc_w_out,
              ml_w_in, ml_b_i, ml_b_f, ml_w_out, ffn_w_gu, ffn_w_down):
    for i in range(DEPTH):
        kind, j = i % N_MIXERS, i // N_MIXERS
        hn = rmsnorm(x, ln_mix_pre[i])
        if kind == 0:
            mixed = stick_breaking_mixer(hn, sb_w_qkv[j], sb_w_o[j])
        elif kind == 1:
            mixed = gated_conv_mixer(hn, gc_w_in[j], gc_conv_w[j], gc_conv_b[j], gc_w_out[j])
        else:
            mixed = mlstm_mixer(hn, ml_w_in[j], ml_b_i[j], ml_b_f[j], ml_w_out[j])
        x = x + rmsnorm(mixed, ln_mix_post[i])
        hn = rmsnorm(x, ln_ffn_pre[i])
        x = x + rmsnorm(swiglu_ffn(hn, ffn_w_gu[i], ffn_w_down[i]), ln_ffn_post[i])
    return x
```

```python
import functools

import jax
import jax.numpy as jnp
from jax import lax
from jax.experimental import pallas as pl
from jax.experimental.pallas import tpu as pltpu

D_MODEL = 1024
DEPTH = 4
N_MIXERS = 3
SB_HEADS = 16
SB_HEAD_DIM = 64
ML_HEADS = 8
ML_QK_DIM = 64
ML_V_DIM = 128
ML_QK_WIDTH = ML_HEADS * ML_QK_DIM
ML_MAIN_WIDTH = 2 * ML_QK_WIDTH + 2 * D_MODEL
ML_CHUNK = 64
D_FF = 2816
RMS_EPS = 1e-6

LANES = 128
ROW_BLOCK = 512
SB_BLOCK = 128
ML_STEP = 2 * ML_CHUNK
FFN_CHUNK = D_FF // 2
VMEM_LIMIT = 56 * 1024 * 1024

F32 = jnp.float32
BF16 = jnp.bfloat16


def _rms(x, g):
    ms = jnp.mean(x * x, axis=-1, keepdims=True)
    return x * lax.rsqrt(ms + RMS_EPS) * g


def _dot(a, b):
    return jnp.dot(a, b, preferred_element_type=F32)


def _dot_nt(a, b):
    return lax.dot_general(a, b, (((1,), (1,)), ((), ())), preferred_element_type=F32)


def _dot_tn(a, b):
    return lax.dot_general(a, b, (((0,), (0,)), ((), ())), preferred_element_type=F32)


def _split3(x):
    p0 = x.astype(BF16)
    r1 = x - p0.astype(F32)
    p1 = r1.astype(BF16)
    p2 = (r1 - p1.astype(F32)).astype(BF16)
    return p0, p1, p2


def _softplus(z):
    return jnp.maximum(z, 0.0) + jnp.log1p(jnp.exp(-jnp.abs(z)))


def _resident(shape):
    nd = len(shape)
    return pl.BlockSpec(shape, lambda *_: (0,) * nd, pipeline_mode=pl.Buffered(1))


def _norm_proj_kernel(x_ref, g_ref, w_ref, o_ref):
    hn = _rms(x_ref[...], g_ref[...]).astype(BF16)
    n = o_ref.shape[-1]
    step = D_MODEL if n % D_MODEL == 0 else n
    for c0 in range(0, n, step):
        o_ref[:, c0:c0 + step] = _dot(hn, w_ref[:, c0:c0 + step]).astype(o_ref.dtype)


def _norm_proj(x2d, g, w, out_dtype):
    m, d = x2d.shape
    n = w.shape[1]
    return pl.pallas_call(
        _norm_proj_kernel,
        grid=(m // ROW_BLOCK,),
        in_specs=[
            pl.BlockSpec((ROW_BLOCK, d), lambda i: (i, 0)),
            _resident((1, d)),
            _resident((d, n)),
        ],
        out_specs=pl.BlockSpec((ROW_BLOCK, n), lambda i: (i, 0)),
        out_shape=jax.ShapeDtypeStruct((m, n), out_dtype),
        compiler_params=pltpu.CompilerParams(
            dimension_semantics=("arbitrary",), vmem_limit_bytes=VMEM_LIMIT),
        name="norm_proj",
    )(x2d, g.reshape(1, d), w)


def _out_ffn_kernel(m_ref, x_ref, wo_ref, gpost_ref, gpre_ref, wgu_ref, wd_ref, gfpost_ref, o_ref):
    mixed = _dot(m_ref[...], wo_ref[...])
    x1 = x_ref[...] + _rms(mixed, gpost_ref[...])
    hn = _rms(x1, gpre_ref[...]).astype(BF16)
    acc = None
    for c0 in range(0, D_FF, FFN_CHUNK):
        g = _dot(hn, wgu_ref[:, c0:c0 + FFN_CHUNK])
        u = _dot(hn, wgu_ref[:, D_FF + c0:D_FF + c0 + FFN_CHUNK])
        a = (g * (1.0 / (1.0 + jnp.exp(-g))) * u).astype(BF16)
        part = _dot(a, wd_ref[c0:c0 + FFN_CHUNK, :])
        acc = part if acc is None else acc + part
    o_ref[...] = x1 + _rms(acc, gfpost_ref[...])


def _out_ffn(mixed2d, x2d, wo, gpost, gpre, wgu, wd, gfpost):
    m, d = x2d.shape
    row = lambda i: (i, 0)
    return pl.pallas_call(
        _out_ffn_kernel,
        grid=(m // ROW_BLOCK,),
        in_specs=[
            pl.BlockSpec((ROW_BLOCK, d), row),
            pl.BlockSpec((ROW_BLOCK, d), row),
            _resident((d, d)),
            _resident((1, d)),
            _resident((1, d)),
            _resident((d, 2 * D_FF)),
            _resident((D_FF, d)),
            _resident((1, d)),
        ],
        out_specs=pl.BlockSpec((ROW_BLOCK, d), row),
        out_shape=jax.ShapeDtypeStruct((m, d), F32),
        compiler_params=pltpu.CompilerParams(
            dimension_semantics=("arbitrary",), vmem_limit_bytes=VMEM_LIMIT),
        name="out_ffn",
    )(mixed2d, x2d, wo, gpost.reshape(1, d), gpre.reshape(1, d), wgu, wd, gfpost.reshape(1, d))


def _sb_tile(q2, kt, vt, tri2, carry, acc, lane_lo, before):
    t = kt.shape[0]
    z = _dot_nt(q2, kt)
    sp = _softplus(z)
    if before is not None:
        sp = jnp.where(before, sp, 0.0)
    hi = sp.astype(BF16)
    lo = (sp - hi.astype(F32)).astype(BF16)
    cs = _dot(jnp.concatenate([hi, lo], axis=1), tri2)
    a = jnp.exp(z - (cs[:, :t] + carry))
    if before is not None:
        a = jnp.where(before, a, 0.0)
    a = a.astype(BF16)
    a_cat = jnp.concatenate([a[:t], a[t:]], axis=1)
    v_cat = jnp.concatenate([jnp.where(lane_lo, vt, 0), jnp.where(lane_lo, 0, vt)], axis=0)
    return carry + cs[:, t:], acc + _dot(a_cat, v_cat)


def _sb_kernel(q_ref, k_ref, v_ref, tri_ref, o_ref):
    t = SB_BLOCK
    qi = pl.program_id(2)
    q = q_ref[0]
    lane_lo = lax.broadcasted_iota(jnp.int32, (t, LANES), 1) < SB_HEAD_DIM
    qs = q * jnp.asarray(SB_HEAD_DIM ** -0.5, BF16)
    q2 = jnp.concatenate([jnp.where(lane_lo, qs, 0), jnp.where(lane_lo, 0, qs)], axis=0)
    tri2 = tri_ref[...]
    row = lax.broadcasted_iota(jnp.int32, (2 * t, t), 0)
    col = lax.broadcasted_iota(jnp.int32, (2 * t, t), 1)
    before = col < jnp.where(row >= t, row - t, row)

    start = pl.multiple_of(qi * t, t)
    carry = jnp.zeros((2 * t, t), F32)
    acc = jnp.zeros((t, LANES), F32)
    carry, acc = _sb_tile(q2, k_ref[0, pl.ds(start, t), :], v_ref[0, pl.ds(start, t), :],
                          tri2, carry, acc, lane_lo, before)

    def body(j, state):
        c, a = state
        s0 = pl.multiple_of((qi - 1 - j) * t, t)
        return _sb_tile(q2, k_ref[0, pl.ds(s0, t), :], v_ref[0, pl.ds(s0, t), :],
                        tri2, c, a, lane_lo, None)

    carry, acc = lax.fori_loop(0, qi, body, (carry, acc))
    o_ref[0] = acc.astype(o_ref.dtype)


def _sb_tri():
    t = SB_BLOCK
    j = lax.broadcasted_iota(jnp.int32, (t, 2 * t), 0)
    s = lax.broadcasted_iota(jnp.int32, (t, 2 * t), 1)
    ext = ((j >= s) | (s >= t)).astype(BF16)
    return jnp.concatenate([ext, ext], axis=0)


def _sb_attention(qkv):
    b, s, _ = qkv.shape
    t = SB_BLOCK
    pairs = SB_HEADS // 2
    return pl.pallas_call(
        _sb_kernel,
        grid=(b, pairs, s // t),
        in_specs=[
            pl.BlockSpec((1, t, LANES), lambda bi, hp, qi: (bi, qi, hp)),
            pl.BlockSpec((1, s, LANES), lambda bi, hp, qi: (bi, 0, pairs + hp)),
            pl.BlockSpec((1, s, LANES), lambda bi, hp, qi: (bi, 0, 2 * pairs + hp)),
            pl.BlockSpec((2 * t, 2 * t), lambda bi, hp, qi: (0, 0)),
        ],
        out_specs=pl.BlockSpec((1, t, LANES), lambda bi, hp, qi: (bi, qi, hp)),
        out_shape=jax.ShapeDtypeStruct((b, s, D_MODEL), BF16),
        compiler_params=pltpu.CompilerParams(
            dimension_semantics=("arbitrary", "arbitrary", "arbitrary")),
        name="sb_attention",
    )(qkv, qkv, qkv, _sb_tri())


HALO = 16


def _conv_kernel(p_ref, h_ref, w_ref, b_ref, o_ref):
    d = D_MODEL
    ts = p_ref.shape[1]
    cu = p_ref[0, :, d:2 * d].astype(F32) * p_ref[0, :, 2 * d:].astype(F32)
    cu_prev = h_ref[0, :, d:2 * d].astype(F32) * h_ref[0, :, 2 * d:].astype(F32)
    cu_prev = jnp.where(pl.program_id(1) == 0, 0.0, cu_prev)
    row = lax.broadcasted_iota(jnp.int32, (HALO, d), 0)

    def shifted(k):
        top = jnp.where(row < k, pltpu.roll(cu_prev, k, axis=0), pltpu.roll(cu[:HALO], k, axis=0))
        rest = pltpu.roll(cu, k, axis=0)[HALO:]
        return jnp.concatenate([top, rest], axis=0)

    y = w_ref[0:1, :] * shifted(2) + w_ref[1:2, :] * shifted(1) + w_ref[2:3, :] * cu + b_ref[...]
    o_ref[0] = (p_ref[0, :, :d].astype(F32) * y).astype(o_ref.dtype)


def _conv_gate(proj, conv_w, conv_b):
    b, s, n = proj.shape
    ts = ROW_BLOCK
    return pl.pallas_call(
        _conv_kernel,
        grid=(b, s // ts),
        in_specs=[
            pl.BlockSpec((1, ts, n), lambda bi, i: (bi, i, 0)),
            pl.BlockSpec((1, HALO, n), lambda bi, i: (bi, jnp.maximum(i * (ts // HALO) - 1, 0), 0)),
            pl.BlockSpec((3, D_MODEL), lambda bi, i: (0, 0)),
            pl.BlockSpec((1, D_MODEL), lambda bi, i: (0, 0)),
        ],
        out_specs=pl.BlockSpec((1, ts, D_MODEL), lambda bi, i: (bi, i, 0)),
        out_shape=jax.ShapeDtypeStruct((b, s, D_MODEL), BF16),
        compiler_params=pltpu.CompilerParams(dimension_semantics=("arbitrary", "arbitrary")),
        name="conv_gate",
    )(proj, proj, conv_w, conv_b.reshape(1, D_MODEL))


def _col(x, h):
    lane = lax.broadcasted_iota(jnp.int32, x.shape, 1)
    return jnp.sum(jnp.where(lane == h, x, 0.0), axis=-1, keepdims=True)


def _mlstm_kernel(p_ref, g_ref, bias_ref, o_ref, st_ref, m_ref):
    L = ML_CHUNK
    step = pl.program_id(1)

    @pl.when(step == 0)
    def _():
        st_ref[...] = jnp.zeros_like(st_ref)
        m_ref[...] = jnp.zeros_like(m_ref)

    gates = g_ref[0] + bias_ref[...]
    lane = lax.broadcasted_iota(jnp.int32, gates.shape, 1)
    is_f = (lane >= ML_HEADS) & (lane < 2 * ML_HEADS)
    lf_all = jnp.where(is_f, jnp.minimum(gates, 0.0) - jnp.log1p(jnp.exp(-jnp.abs(gates))), 0.0)

    r = lax.broadcasted_iota(jnp.int32, (L, L), 0)
    c = lax.broadcasted_iota(jnp.int32, (L, L), 1)
    causal = c <= r
    tril = causal.astype(BF16)
    eye = (lax.broadcasted_iota(jnp.int32, (LANES, LANES), 0)
           == lax.broadcasted_iota(jnp.int32, (LANES, LANES), 1)).astype(BF16)
    lane_lo = lax.broadcasted_iota(jnp.int32, (L, LANES), 1) < ML_QK_DIM
    ones_col = (lax.broadcasted_iota(jnp.int32, (L, LANES), 1) == 0).astype(BF16)

    for ck in range(ML_STEP // L):
        rows = slice(ck * L, (ck + 1) * L)
        ig_c = gates[rows]
        lf_c = lf_all[rows]
        b_c = sum(_dot(tril, p) for p in _split3(lf_c))
        lane_c = lax.broadcasted_iota(jnp.int32, (L, LANES), 1)
        src = jnp.where(lane_c >= ML_HEADS, b_c, ig_c)
        src_t = sum(_dot_nt(eye, jnp.concatenate([p, jnp.zeros_like(p)], axis=0))
                    for p in _split3(src))[:, :L]
        m_all = m_ref[...]
        b_last_all = b_c[L - 1:L, :]
        ig_sh = pltpu.roll(ig_c, ML_HEADS, axis=1)
        log_w = b_last_all - b_c + ig_sh
        m_new = jnp.maximum(b_last_all + m_all, jnp.max(log_w, axis=0, keepdims=True))
        w_all = jnp.exp(log_w - m_new)
        decay_all = jnp.exp(b_last_all + m_all - m_new)
        m_inter_all = b_c + m_all

        for h in range(ML_HEADS):
            hp, odd = h // 2, h % 2
            own = lane_lo != bool(odd)
            qh = jnp.where(own, p_ref[0, rows, hp * LANES:(hp + 1) * LANES], 0)
            kh = jnp.where(own, p_ref[0, rows, ML_QK_WIDTH + hp * LANES:ML_QK_WIDTH + (hp + 1) * LANES], 0)
            kh = kh * jnp.asarray(ML_QK_DIM ** -0.5, BF16)
            vh = p_ref[0, rows, 2 * ML_QK_WIDTH + h * LANES:2 * ML_QK_WIDTH + (h + 1) * LANES]
            oh = p_ref[0, rows, 2 * ML_QK_WIDTH + D_MODEL + h * LANES:
                       2 * ML_QK_WIDTH + D_MODEL + (h + 1) * LANES].astype(F32)
            v_ext = jnp.concatenate([vh, ones_col], axis=1)

            b_col = _col(b_c, ML_HEADS + h)
            ig_row = src_t[h:h + 1, :]
            b_row = src_t[ML_HEADS + h:ML_HEADS + h + 1, :]
            log_d = jnp.where(causal, b_col - b_row + ig_row, -jnp.inf)
            m_inter = _col(m_inter_all, ML_HEADS + h)
            m_t = jnp.maximum(m_inter, jnp.max(log_d, axis=-1, keepdims=True))
            d_mat = jnp.exp(log_d - m_t)
            inter = jnp.exp(m_inter - m_t)

            s_mat = _dot_nt(qh, kh) * d_mat
            st = st_ref[h]
            sv = _dot(s_mat.astype(BF16), v_ext)
            qc = _dot(qh, st.astype(BF16))
            tot = sv + inter * qc
            den = tot[:, LANES:LANES + 1]
            h_out = tot[:, :LANES] / jnp.maximum(jnp.abs(den), jnp.exp(-m_t))
            o_ref[0, rows, h * LANES:(h + 1) * LANES] = (
                (1.0 / (1.0 + jnp.exp(-oh))) * h_out).astype(o_ref.dtype)

            w_col = _col(w_all, ML_HEADS + h)
            decay = _col(decay_all, ML_HEADS + h)
            kw = (kh.astype(F32) * w_col).astype(BF16)
            st_ref[h] = decay * st + _dot_tn(kw, v_ext)
        m_ref[...] = m_new


def _mlstm_core(main, gates, bias_row):
    b, s, n = main.shape
    return pl.pallas_call(
        _mlstm_kernel,
        grid=(b, s // ML_STEP),
        in_specs=[
            pl.BlockSpec((1, ML_STEP, n), lambda bi, i: (bi, i, 0)),
            pl.BlockSpec((1, ML_STEP, LANES), lambda bi, i: (bi, i, 0)),
            pl.BlockSpec((1, LANES), lambda bi, i: (0, 0)),
        ],
        out_specs=pl.BlockSpec((1, ML_STEP, D_MODEL), lambda bi, i: (bi, i, 0)),
        out_shape=jax.ShapeDtypeStruct((b, s, D_MODEL), BF16),
        scratch_shapes=[
            pltpu.VMEM((ML_HEADS, LANES, 2 * LANES), F32),
            pltpu.VMEM((1, LANES), F32),
        ],
        compiler_params=pltpu.CompilerParams(dimension_semantics=("arbitrary", "arbitrary")),
        name="mlstm_core",
    )(main, gates, bias_row)


def kernel(x, ln_mix_pre, ln_mix_post, ln_ffn_pre, ln_ffn_post, sb_w_qkv, sb_w_o, gc_w_in, gc_conv_w,
           gc_conv_b, gc_w_out, ml_w_in, ml_b_i, ml_b_f, ml_w_out, ffn_w_gu, ffn_w_down):
    bsz, seq, d = x.shape
    x2d = x.reshape(bsz * seq, d)
    for i in range(DEPTH):
        kind, j = i % N_MIXERS, i // N_MIXERS
        if kind == 0:
            qkv = _norm_proj(x2d, ln_mix_pre[i], sb_w_qkv[j].astype(BF16), BF16)
            mixed = _sb_attention(qkv.reshape(bsz, seq, 3 * d))
            w_out = sb_w_o[j]
        elif kind == 1:
            proj = _norm_proj(x2d, ln_mix_pre[i], gc_w_in[j].astype(BF16), BF16)
            mixed = _conv_gate(proj.reshape(bsz, seq, 3 * d), gc_conv_w[j], gc_conv_b[j])
            w_out = gc_w_out[j]
        else:
            w_in = ml_w_in[j]
            w_gate = jnp.pad(w_in[:, ML_MAIN_WIDTH:], ((0, 0), (0, LANES - 2 * ML_HEADS)))
            main = _norm_proj(x2d, ln_mix_pre[i], w_in[:, :ML_MAIN_WIDTH].astype(BF16), BF16)
            gates = _norm_proj(x2d, ln_mix_pre[i], w_gate.astype(BF16), F32)
            bias_row = jnp.pad(jnp.concatenate([ml_b_i[j], ml_b_f[j]]), (0, LANES - 2 * ML_HEADS))
            mixed = _mlstm_core(main.reshape(bsz, seq, ML_MAIN_WIDTH), gates.reshape(bsz, seq, LANES),
                                bias_row.reshape(1, LANES))
            w_out = ml_w_out[j]
        x2d = _out_ffn(mixed.reshape(bsz * seq, d), x2d, w_out.astype(BF16), ln_mix_post[i],
                       ln_ffn_pre[i], ffn_w_gu[i].astype(BF16), ffn_w_down[i].astype(BF16),
                       ln_ffn_post[i])
    return x2d.reshape(bsz, seq, d)
```

```python
import functools

import jax
import jax.numpy as jnp
from jax import lax
from jax.experimental import pallas as pl
from jax.experimental.pallas import tpu as pltpu

D_MODEL = 1024
DEPTH = 4
N_MIXERS = 3
SB_HEADS = 16
SB_HEAD_DIM = 64
ML_HEADS = 8
ML_QK_DIM = 64
ML_V_DIM = 128
ML_QK_WIDTH = ML_HEADS * ML_QK_DIM
ML_MAIN_WIDTH = 2 * ML_QK_WIDTH + 2 * D_MODEL
ML_CHUNK = 64
D_FF = 2816
RMS_EPS = 1e-6

LANES = 128
ROW_BLOCK = 512
SB_BLOCK = 128
ML_STEP = 2 * ML_CHUNK
FFN_CHUNK = D_FF // 2
VMEM_LIMIT = 56 * 1024 * 1024

F32 = jnp.float32
BF16 = jnp.bfloat16


def _rms(x, g):
    ms = jnp.mean(x * x, axis=-1, keepdims=True)
    return x * lax.rsqrt(ms + RMS_EPS) * g


def _dot(a, b):
    return jnp.dot(a, b, preferred_element_type=F32)


def _dot_nt(a, b):
    return lax.dot_general(a, b, (((1,), (1,)), ((), ())), preferred_element_type=F32)


def _dot_tn(a, b):
    return lax.dot_general(a, b, (((0,), (0,)), ((), ())), preferred_element_type=F32)


def _split3(x):
    p0 = x.astype(BF16)
    r1 = x - p0.astype(F32)
    p1 = r1.astype(BF16)
    p2 = (r1 - p1.astype(F32)).astype(BF16)
    return p0, p1, p2


def _softplus(z):
    return jnp.maximum(z, 0.0) + jnp.log1p(jnp.exp(-jnp.abs(z)))


def _resident(shape):
    nd = len(shape)
    return pl.BlockSpec(shape, lambda *_: (0,) * nd, pipeline_mode=pl.Buffered(1))


def _norm_proj_kernel(x_ref, g_ref, w_ref, o_ref):
    hn = _rms(x_ref[...], g_ref[...]).astype(BF16)
    n = o_ref.shape[-1]
    step = D_MODEL if n % D_MODEL == 0 else n
    for c0 in range(0, n, step):
        o_ref[:, c0:c0 + step] = _dot(hn, w_ref[:, c0:c0 + step]).astype(o_ref.dtype)


def _norm_proj(x2d, g, w, out_dtype):
    m, d = x2d.shape
    n = w.shape[1]
    return pl.pallas_call(
        _norm_proj_kernel,
        grid=(m // ROW_BLOCK,),
        in_specs=[
            pl.BlockSpec((ROW_BLOCK, d), lambda i: (i, 0)),
            _resident((1, d)),
            _resident((d, n)),
        ],
        out_specs=pl.BlockSpec((ROW_BLOCK, n), lambda i: (i, 0)),
        out_shape=jax.ShapeDtypeStruct((m, n), out_dtype),
        compiler_params=pltpu.CompilerParams(
            dimension_semantics=("arbitrary",), vmem_limit_bytes=VMEM_LIMIT),
        name="norm_proj",
    )(x2d, g.reshape(1, d), w)


def _out_ffn_kernel(m_ref, x_ref, wo_ref, gpost_ref, gpre_ref, wgu_ref, wd_ref, gfpost_ref, o_ref):
    mixed = _dot(m_ref[...], wo_ref[...])
    x1 = x_ref[...] + _rms(mixed, gpost_ref[...])
    hn = _rms(x1, gpre_ref[...]).astype(BF16)
    acc = None
    for c0 in range(0, D_FF, FFN_CHUNK):
        g = _dot(hn, wgu_ref[:, c0:c0 + FFN_CHUNK])
        u = _dot(hn, wgu_ref[:, D_FF + c0:D_FF + c0 + FFN_CHUNK])
        a = (g * (1.0 / (1.0 + jnp.exp(-g))) * u).astype(BF16)
        part = _dot(a, wd_ref[c0:c0 + FFN_CHUNK, :])
        acc = part if acc is None else acc + part
    o_ref[...] = x1 + _rms(acc, gfpost_ref[...])


def _out_ffn(mixed2d, x2d, wo, gpost, gpre, wgu, wd, gfpost):
    m, d = x2d.shape
    row = lambda i: (i, 0)
    return pl.pallas_call(
        _out_ffn_kernel,
        grid=(m // ROW_BLOCK,),
        in_specs=[
            pl.BlockSpec((ROW_BLOCK, d), row),
            pl.BlockSpec((ROW_BLOCK, d), row),
            _resident((d, d)),
            _resident((1, d)),
            _resident((1, d)),
            _resident((d, 2 * D_FF)),
            _resident((D_FF, d)),
            _resident((1, d)),
        ],
        out_specs=pl.BlockSpec((ROW_BLOCK, d), row),
        out_shape=jax.ShapeDtypeStruct((m, d), F32),
        compiler_params=pltpu.CompilerParams(
            dimension_semantics=("arbitrary",), vmem_limit_bytes=VMEM_LIMIT),
        name="out_ffn",
    )(mixed2d, x2d, wo, gpost.reshape(1, d), gpre.reshape(1, d), wgu, wd, gfpost.reshape(1, d))


SB_PAIRS = SB_HEADS // 2
SB_SKIP_MASS = 100.0


def _sb_tile(q2, kt, vt, tri2, carry, acc, lane_lo, before):
    t = kt.shape[0]
    z = _dot_nt(q2, kt)
    sp = jnp.maximum(z, 0.0) + jnp.log(1.0 + jnp.exp(-jnp.abs(z)))
    if before is not None:
        sp = jnp.where(before, sp, 0.0)
    hi = sp.astype(BF16)
    lo = (sp - hi.astype(F32)).astype(BF16)
    cs = _dot(jnp.concatenate([hi, lo], axis=1), tri2)
    a = jnp.exp(z - (cs[:, :t] + carry))
    if before is not None:
        a = jnp.where(before, a, 0.0)
    a = a.astype(BF16)
    a_cat = jnp.concatenate([a[:t], a[t:]], axis=1)
    v_cat = jnp.concatenate([jnp.where(lane_lo, vt, 0), jnp.where(lane_lo, 0, vt)], axis=0)
    return carry + cs[:, t:], acc + _dot(a_cat, v_cat)


def _sb_kernel(q_ref, k_ref, v_ref, tri_ref, o_ref, q2_ref, carry_ref, acc_ref):
    t = SB_BLOCK
    qi = pl.program_id(1)
    lane_lo = lax.broadcasted_iota(jnp.int32, (t, LANES), 1) < SB_HEAD_DIM
    row = lax.broadcasted_iota(jnp.int32, (2 * t, t), 0)
    col = lax.broadcasted_iota(jnp.int32, (2 * t, t), 1)
    before = col < jnp.where(row >= t, row - t, row)

    for hp in range(SB_PAIRS):
        qs = q_ref[0, :, hp * LANES:(hp + 1) * LANES] * jnp.asarray(SB_HEAD_DIM ** -0.5, BF16)
        q2_ref[hp] = jnp.concatenate([jnp.where(lane_lo, qs, 0), jnp.where(lane_lo, 0, qs)], axis=0)
    carry_ref[...] = jnp.zeros_like(carry_ref)
    acc_ref[...] = jnp.zeros_like(acc_ref)

    def sweep(j, mask):
        s0 = pl.multiple_of((qi - j) * t, t)
        least = None
        for hp in range(SB_PAIRS):
            cols = slice(hp * LANES, (hp + 1) * LANES)
            carry, acc = _sb_tile(q2_ref[hp], k_ref[0, pl.ds(s0, t), cols], v_ref[0, pl.ds(s0, t), cols],
                                  tri_ref[...], carry_ref[hp], acc_ref[hp], lane_lo, mask)
            carry_ref[hp] = carry
            acc_ref[hp] = acc
            least = carry if least is None else jnp.minimum(least, carry)
        return jnp.min(least)

    least0 = sweep(0, before)
    lax.while_loop(lambda st: (st[0] <= qi) & (st[1] < SB_SKIP_MASS),
                   lambda st: (st[0] + 1, sweep(st[0], None)),
                   (jnp.int32(1), least0))
    for hp in range(SB_PAIRS):
        o_ref[0, :, hp * LANES:(hp + 1) * LANES] = acc_ref[hp].astype(o_ref.dtype)


def _sb_tri():
    t = SB_BLOCK
    j = lax.broadcasted_iota(jnp.int32, (t, 2 * t), 0)
    s = lax.broadcasted_iota(jnp.int32, (t, 2 * t), 1)
    ext = ((j >= s) | (s >= t)).astype(BF16)
    return jnp.concatenate([ext, ext], axis=0)


def _sb_attention(qkv):
    b, s, _ = qkv.shape
    t = SB_BLOCK
    return pl.pallas_call(
        _sb_kernel,
        grid=(b, s // t),
        in_specs=[
            pl.BlockSpec((1, t, D_MODEL), lambda bi, qi: (bi, qi, 0)),
            pl.BlockSpec((1, s, D_MODEL), lambda bi, qi: (bi, 0, 1)),
            pl.BlockSpec((1, s, D_MODEL), lambda bi, qi: (bi, 0, 2)),
            pl.BlockSpec((2 * t, 2 * t), lambda bi, qi: (0, 0)),
        ],
        out_specs=pl.BlockSpec((1, t, D_MODEL), lambda bi, qi: (bi, qi, 0)),
        out_shape=jax.ShapeDtypeStruct((b, s, D_MODEL), BF16),
        scratch_shapes=[
            pltpu.VMEM((SB_PAIRS, 2 * t, LANES), BF16),
            pltpu.VMEM((SB_PAIRS, 2 * t, t), F32),
            pltpu.VMEM((SB_PAIRS, t, LANES), F32),
        ],
        compiler_params=pltpu.CompilerParams(
            dimension_semantics=("arbitrary", "arbitrary"), vmem_limit_bytes=VMEM_LIMIT),
        name="sb_attention",
    )(qkv, qkv, qkv, _sb_tri())


HALO = 16


def _conv_kernel(p_ref, h_ref, w_ref, b_ref, o_ref):
    d = D_MODEL
    ts = p_ref.shape[1]
    cu = p_ref[0, :, d:2 * d].astype(F32) * p_ref[0, :, 2 * d:].astype(F32)
    cu_prev = h_ref[0, :, d:2 * d].astype(F32) * h_ref[0, :, 2 * d:].astype(F32)
    cu_prev = jnp.where(pl.program_id(1) == 0, 0.0, cu_prev)
    row = lax.broadcasted_iota(jnp.int32, (HALO, d), 0)

    def shifted(k):
        top = jnp.where(row < k, pltpu.roll(cu_prev, k, axis=0), pltpu.roll(cu[:HALO], k, axis=0))
        rest = pltpu.roll(cu, k, axis=0)[HALO:]
        return jnp.concatenate([top, rest], axis=0)

    y = w_ref[0:1, :] * shifted(2) + w_ref[1:2, :] * shifted(1) + w_ref[2:3, :] * cu + b_ref[...]
    o_ref[0] = (p_ref[0, :, :d].astype(F32) * y).astype(o_ref.dtype)


def _conv_gate(proj, conv_w, conv_b):
    b, s, n = proj.shape
    ts = ROW_BLOCK
    return pl.pallas_call(
        _conv_kernel,
        grid=(b, s // ts),
        in_specs=[
            pl.BlockSpec((1, ts, n), lambda bi, i: (bi, i, 0)),
            pl.BlockSpec((1, HALO, n), lambda bi, i: (bi, jnp.maximum(i * (ts // HALO) - 1, 0), 0)),
            pl.BlockSpec((3, D_MODEL), lambda bi, i: (0, 0)),
            pl.BlockSpec((1, D_MODEL), lambda bi, i: (0, 0)),
        ],
        out_specs=pl.BlockSpec((1, ts, D_MODEL), lambda bi, i: (bi, i, 0)),
        out_shape=jax.ShapeDtypeStruct((b, s, D_MODEL), BF16),
        compiler_params=pltpu.CompilerParams(dimension_semantics=("arbitrary", "arbitrary")),
        name="conv_gate",
    )(proj, proj, conv_w, conv_b.reshape(1, D_MODEL))


def _col(x, h):
    lane = lax.broadcasted_iota(jnp.int32, x.shape, 1)
    return jnp.sum(jnp.where(lane == h, x, 0.0), axis=-1, keepdims=True)


def _mlstm_kernel(p_ref, g_ref, bias_ref, o_ref, st_ref, m_ref):
    L = ML_CHUNK
    step = pl.program_id(1)

    @pl.when(step == 0)
    def _():
        st_ref[...] = jnp.zeros_like(st_ref)
        m_ref[...] = jnp.zeros_like(m_ref)

    gates = g_ref[0] + bias_ref[...]
    lane = lax.broadcasted_iota(jnp.int32, gates.shape, 1)
    is_f = (lane >= ML_HEADS) & (lane < 2 * ML_HEADS)
    lf_all = jnp.where(is_f, jnp.minimum(gates, 0.0) - jnp.log1p(jnp.exp(-jnp.abs(gates))), 0.0)

    r = lax.broadcasted_iota(jnp.int32, (L, L), 0)
    c = lax.broadcasted_iota(jnp.int32, (L, L), 1)
    causal = c <= r
    tril = causal.astype(BF16)
    eye = (lax.broadcasted_iota(jnp.int32, (LANES, LANES), 0)
           == lax.broadcasted_iota(jnp.int32, (LANES, LANES), 1)).astype(BF16)
    lane_lo = lax.broadcasted_iota(jnp.int32, (L, LANES), 1) < ML_QK_DIM
    ones_col = (lax.broadcasted_iota(jnp.int32, (L, LANES), 1) == 0).astype(BF16)

    for ck in range(ML_STEP // L):
        rows = slice(ck * L, (ck + 1) * L)
        ig_c = gates[rows]
        lf_c = lf_all[rows]
        b_c = sum(_dot(tril, p) for p in _split3(lf_c))
        lane_c = lax.broadcasted_iota(jnp.int32, (L, LANES), 1)
        src = jnp.where(lane_c >= ML_HEADS, b_c, ig_c)
        src_t = sum(_dot_nt(eye, jnp.concatenate([p, jnp.zeros_like(p)], axis=0))
                    for p in _split3(src))[:, :L]
        m_all = m_ref[...]
        b_last_all = b_c[L - 1:L, :]
        ig_sh = pltpu.roll(ig_c, ML_HEADS, axis=1)
        log_w = b_last_all - b_c + ig_sh
        m_new = jnp.maximum(b_last_all + m_all, jnp.max(log_w, axis=0, keepdims=True))
        w_all = jnp.exp(log_w - m_new)
        decay_all = jnp.exp(b_last_all + m_all - m_new)
        m_inter_all = b_c + m_all

        for h in range(ML_HEADS):
            hp, odd = h // 2, h % 2
            own = lane_lo != bool(odd)
            qh = jnp.where(own, p_ref[0, rows, hp * LANES:(hp + 1) * LANES], 0)
            kh = jnp.where(own, p_ref[0, rows, ML_QK_WIDTH + hp * LANES:ML_QK_WIDTH + (hp + 1) * LANES], 0)
            kh = kh * jnp.asarray(ML_QK_DIM ** -0.5, BF16)
            vh = p_ref[0, rows, 2 * ML_QK_WIDTH + h * LANES:2 * ML_QK_WIDTH + (h + 1) * LANES]
            oh = p_ref[0, rows, 2 * ML_QK_WIDTH + D_MODEL + h * LANES:
                       2 * ML_QK_WIDTH + D_MODEL + (h + 1) * LANES].astype(F32)
            v_ext = jnp.concatenate([vh, ones_col], axis=1)

            b_col = _col(b_c, ML_HEADS + h)
            ig_row = src_t[h:h + 1, :]
            b_row = src_t[ML_HEADS + h:ML_HEADS + h + 1, :]
            log_d = jnp.where(causal, b_col - b_row + ig_row, -jnp.inf)
            m_inter = _col(m_inter_all, ML_HEADS + h)
            m_t = jnp.maximum(m_inter, jnp.max(log_d, axis=-1, keepdims=True))
            d_mat = jnp.exp(log_d - m_t)
            inter = jnp.exp(m_inter - m_t)

            s_mat = _dot_nt(qh, kh) * d_mat
            st = st_ref[h]
            sv = _dot(s_mat.astype(BF16), v_ext)
            qc = _dot(qh, st.astype(BF16))
            tot = sv + inter * qc
            den = tot[:, LANES:LANES + 1]
            h_out = tot[:, :LANES] / jnp.maximum(jnp.abs(den), jnp.exp(-m_t))
            o_ref[0, rows, h * LANES:(h + 1) * LANES] = (
                (1.0 / (1.0 + jnp.exp(-oh))) * h_out).astype(o_ref.dtype)

            w_col = _col(w_all, ML_HEADS + h)
            decay = _col(decay_all, ML_HEADS + h)
            kw = (kh.astype(F32) * w_col).astype(BF16)
            st_ref[h] = decay * st + _dot_tn(kw, v_ext)
        m_ref[...] = m_new


def _mlstm_core(main, gates, bias_row):
    b, s, n = main.shape
    return pl.pallas_call(
        _mlstm_kernel,
        grid=(b, s // ML_STEP),
        in_specs=[
            pl.BlockSpec((1, ML_STEP, n), lambda bi, i: (bi, i, 0)),
            pl.BlockSpec((1, ML_STEP, LANES), lambda bi, i: (bi, i, 0)),
            pl.BlockSpec((1, LANES), lambda bi, i: (0, 0)),
        ],
        out_specs=pl.BlockSpec((1, ML_STEP, D_MODEL), lambda bi, i: (bi, i, 0)),
        out_shape=jax.ShapeDtypeStruct((b, s, D_MODEL), BF16),
        scratch_shapes=[
            pltpu.VMEM((ML_HEADS, LANES, 2 * LANES), F32),
            pltpu.VMEM((1, LANES), F32),
        ],
        compiler_params=pltpu.CompilerParams(dimension_semantics=("arbitrary", "arbitrary")),
        name="mlstm_core",
    )(main, gates, bias_row)


def kernel(x, ln_mix_pre, ln_mix_post, ln_ffn_pre, ln_ffn_post, sb_w_qkv, sb_w_o, gc_w_in, gc_conv_w,
           gc_conv_b, gc_w_out, ml_w_in, ml_b_i, ml_b_f, ml_w_out, ffn_w_gu, ffn_w_down):
    bsz, seq, d = x.shape
    x2d = x.reshape(bsz * seq, d)
    for i in range(DEPTH):
        kind, j = i % N_MIXERS, i // N_MIXERS
        if kind == 0:
            qkv = _norm_proj(x2d, ln_mix_pre[i], sb_w_qkv[j].astype(BF16), BF16)
            mixed = _sb_attention(qkv.reshape(bsz, seq, 3 * d))
            w_out = sb_w_o[j]
        elif kind == 1:
            proj = _norm_proj(x2d, ln_mix_pre[i], gc_w_in[j].astype(BF16), BF16)
            mixed = _conv_gate(proj.reshape(bsz, seq, 3 * d), gc_conv_w[j], gc_conv_b[j])
            w_out = gc_w_out[j]
        else:
            w_in = ml_w_in[j]
            w_gate = jnp.pad(w_in[:, ML_MAIN_WIDTH:], ((0, 0), (0, LANES - 2 * ML_HEADS)))
            main = _norm_proj(x2d, ln_mix_pre[i], w_in[:, :ML_MAIN_WIDTH].astype(BF16), BF16)
            gates = _norm_proj(x2d, ln_mix_pre[i], w_gate.astype(BF16), F32)
            bias_row = jnp.pad(jnp.concatenate([ml_b_i[j], ml_b_f[j]]), (0, LANES - 2 * ML_HEADS))
            mixed = _mlstm_core(main.reshape(bsz, seq, ML_MAIN_WIDTH), gates.reshape(bsz, seq, LANES),
                                bias_row.reshape(1, LANES))
            w_out = ml_w_out[j]
        x2d = _out_ffn(mixed.reshape(bsz * seq, d), x2d, w_out.astype(BF16), ln_mix_post[i],
                       ln_ffn_pre[i], ffn_w_gu[i].astype(BF16), ffn_w_down[i].astype(BF16),
                       ln_ffn_post[i])
    return x2d.reshape(bsz, seq, d)
```

```python
import functools

import jax
import jax.numpy as jnp
from jax import lax
from jax.experimental import pallas as pl
from jax.experimental.pallas import tpu as pltpu

D_MODEL = 1024
DEPTH = 4
N_MIXERS = 3
SB_HEADS = 16
SB_HEAD_DIM = 64
ML_HEADS = 8
ML_QK_DIM = 64
ML_V_DIM = 128
ML_QK_WIDTH = ML_HEADS * ML_QK_DIM
ML_MAIN_WIDTH = 2 * ML_QK_WIDTH + 2 * D_MODEL
ML_CHUNK = 64
D_FF = 2816
RMS_EPS = 1e-6

LANES = 128
ROW_BLOCK = 512
SB_BLOCK = 128
ML_STEP = 2 * ML_CHUNK
FFN_CHUNK = D_FF // 2
VMEM_LIMIT = 56 * 1024 * 1024

F32 = jnp.float32
BF16 = jnp.bfloat16


def _rms(x, g):
    ms = jnp.mean(x * x, axis=-1, keepdims=True)
    return x * lax.rsqrt(ms + RMS_EPS) * g


def _dot(a, b):
    return jnp.dot(a, b, preferred_element_type=F32)


def _dot_nt(a, b):
    return lax.dot_general(a, b, (((1,), (1,)), ((), ())), preferred_element_type=F32)


def _dot_tn(a, b):
    return lax.dot_general(a, b, (((0,), (0,)), ((), ())), preferred_element_type=F32)


def _split3(x):
    p0 = x.astype(BF16)
    r1 = x - p0.astype(F32)
    p1 = r1.astype(BF16)
    p2 = (r1 - p1.astype(F32)).astype(BF16)
    return p0, p1, p2


def _resident(shape):
    nd = len(shape)
    return pl.BlockSpec(shape, lambda *_: (0,) * nd, pipeline_mode=pl.Buffered(1))


def _norm_proj_kernel(x_ref, g_ref, w_ref, o_ref):
    hn = _rms(x_ref[...], g_ref[...]).astype(BF16)
    n = o_ref.shape[-1]
    step = D_MODEL if n % D_MODEL == 0 else n
    for c0 in range(0, n, step):
        o_ref[:, c0:c0 + step] = _dot(hn, w_ref[:, c0:c0 + step]).astype(o_ref.dtype)


def _norm_proj(x2d, g, w, out_dtype):
    m, d = x2d.shape
    n = w.shape[1]
    return pl.pallas_call(
        _norm_proj_kernel,
        grid=(m // ROW_BLOCK,),
        in_specs=[
            pl.BlockSpec((ROW_BLOCK, d), lambda i: (i, 0)),
            _resident((1, d)),
            _resident((d, n)),
        ],
        out_specs=pl.BlockSpec((ROW_BLOCK, n), lambda i: (i, 0)),
        out_shape=jax.ShapeDtypeStruct((m, n), out_dtype),
        compiler_params=pltpu.CompilerParams(
            dimension_semantics=("arbitrary",), vmem_limit_bytes=VMEM_LIMIT),
        name="norm_proj",
    )(x2d, g.reshape(1, d), w)


def _out_ffn_kernel(m_ref, x_ref, wo_ref, gpost_ref, gpre_ref, wgu_ref, wd_ref, gfpost_ref, o_ref):
    mixed = _dot(m_ref[...], wo_ref[...])
    x1 = x_ref[...] + _rms(mixed, gpost_ref[...])
    hn = _rms(x1, gpre_ref[...]).astype(BF16)
    acc = None
    for c0 in range(0, D_FF, FFN_CHUNK):
        g = _dot(hn, wgu_ref[:, c0:c0 + FFN_CHUNK])
        u = _dot(hn, wgu_ref[:, D_FF + c0:D_FF + c0 + FFN_CHUNK])
        a = (g * (1.0 / (1.0 + jnp.exp(-g))) * u).astype(BF16)
        part = _dot(a, wd_ref[c0:c0 + FFN_CHUNK, :])
        acc = part if acc is None else acc + part
    o_ref[...] = x1 + _rms(acc, gfpost_ref[...])


def _out_ffn(mixed2d, x2d, wo, gpost, gpre, wgu, wd, gfpost):
    m, d = x2d.shape
    row = lambda i: (i, 0)
    return pl.pallas_call(
        _out_ffn_kernel,
        grid=(m // ROW_BLOCK,),
        in_specs=[
            pl.BlockSpec((ROW_BLOCK, d), row),
            pl.BlockSpec((ROW_BLOCK, d), row),
            _resident((d, d)),
            _resident((1, d)),
            _resident((1, d)),
            _resident((d, 2 * D_FF)),
            _resident((D_FF, d)),
            _resident((1, d)),
        ],
        out_specs=pl.BlockSpec((ROW_BLOCK, d), row),
        out_shape=jax.ShapeDtypeStruct((m, d), F32),
        compiler_params=pltpu.CompilerParams(
            dimension_semantics=("arbitrary",), vmem_limit_bytes=VMEM_LIMIT),
        name="out_ffn",
    )(mixed2d, x2d, wo, gpost.reshape(1, d), gpre.reshape(1, d), wgu, wd, gfpost.reshape(1, d))


SB_PAIRS = SB_HEADS // 2
SB_SKIP_MASS = 100.0


def _sb_kernel(q_ref, k_ref, v_ref, tri_ref, o_ref, q2_ref, carry_ref, acc_ref):
    t = SB_BLOCK
    qi = pl.program_id(1)
    lane_lo = lax.broadcasted_iota(jnp.int32, (t, LANES), 1) < SB_HEAD_DIM
    row = lax.broadcasted_iota(jnp.int32, (2 * t, t), 0)
    col = lax.broadcasted_iota(jnp.int32, (2 * t, t), 1)
    before = col < jnp.where(row >= t, row - t, row)

    for hp in range(SB_PAIRS):
        qs = q_ref[0, :, hp * LANES:(hp + 1) * LANES] * jnp.asarray(SB_HEAD_DIM ** -0.5, BF16)
        q2_ref[hp] = jnp.concatenate([jnp.where(lane_lo, qs, 0), jnp.where(lane_lo, 0, qs)], axis=0)
    carry_ref[...] = jnp.zeros_like(carry_ref)
    acc_ref[...] = jnp.zeros_like(acc_ref)

    def sweep(j, mask):
        s0 = pl.multiple_of((qi - j) * t, t)
        pairs = range(SB_PAIRS)
        cols = [slice(hp * LANES, (hp + 1) * LANES) for hp in pairs]
        zs = [_dot_nt(q2_ref[hp], k_ref[0, pl.ds(s0, t), cols[hp]]) for hp in pairs]
        sps = []
        for hp in pairs:
            z = zs[hp]
            sp = jnp.maximum(z, 0.0) + jnp.log(1.0 + jnp.exp(-jnp.abs(z)))
            if mask is not None:
                sp = jnp.where(mask, sp, 0.0)
            hi = sp.astype(BF16)
            lo = (sp - hi.astype(F32)).astype(BF16)
            sps.append(jnp.concatenate([hi, lo], axis=1))
        css = [_dot(sps[hp], tri_ref[...]) for hp in pairs]
        a_cats = []
        least = None
        for hp in pairs:
            carry = carry_ref[hp]
            a = jnp.exp(zs[hp] - (css[hp][:, :t] + carry))
            if mask is not None:
                a = jnp.where(mask, a, 0.0)
            a = a.astype(BF16)
            a_cats.append(jnp.concatenate([a[:t], a[t:]], axis=1))
            carry = carry + css[hp][:, t:]
            carry_ref[hp] = carry
            least = carry if least is None else jnp.minimum(least, carry)
        for hp in pairs:
            vt = v_ref[0, pl.ds(s0, t), cols[hp]]
            v_cat = jnp.concatenate([jnp.where(lane_lo, vt, 0), jnp.where(lane_lo, 0, vt)], axis=0)
            acc_ref[hp] += _dot(a_cats[hp], v_cat)
        return jnp.min(least)

    least0 = sweep(0, before)
    lax.while_loop(lambda st: (st[0] <= qi) & (st[1] < SB_SKIP_MASS),
                   lambda st: (st[0] + 1, sweep(st[0], None)),
                   (jnp.int32(1), least0))
    for hp in range(SB_PAIRS):
        o_ref[0, :, hp * LANES:(hp + 1) * LANES] = acc_ref[hp].astype(o_ref.dtype)


def _sb_tri():
    t = SB_BLOCK
    j = lax.broadcasted_iota(jnp.int32, (t, 2 * t), 0)
    s = lax.broadcasted_iota(jnp.int32, (t, 2 * t), 1)
    ext = ((j >= s) | (s >= t)).astype(BF16)
    return jnp.concatenate([ext, ext], axis=0)


def _sb_attention(qkv):
    b, s, _ = qkv.shape
    t = SB_BLOCK
    return pl.pallas_call(
        _sb_kernel,
        grid=(b, s // t),
        in_specs=[
            pl.BlockSpec((1, t, D_MODEL), lambda bi, qi: (bi, qi, 0)),
            pl.BlockSpec((1, s, D_MODEL), lambda bi, qi: (bi, 0, 1)),
            pl.BlockSpec((1, s, D_MODEL), lambda bi, qi: (bi, 0, 2)),
            pl.BlockSpec((2 * t, 2 * t), lambda bi, qi: (0, 0)),
        ],
        out_specs=pl.BlockSpec((1, t, D_MODEL), lambda bi, qi: (bi, qi, 0)),
        out_shape=jax.ShapeDtypeStruct((b, s, D_MODEL), BF16),
        scratch_shapes=[
            pltpu.VMEM((SB_PAIRS, 2 * t, LANES), BF16),
            pltpu.VMEM((SB_PAIRS, 2 * t, t), F32),
            pltpu.VMEM((SB_PAIRS, t, LANES), F32),
        ],
        compiler_params=pltpu.CompilerParams(
            dimension_semantics=("arbitrary", "arbitrary"), vmem_limit_bytes=VMEM_LIMIT),
        name="sb_attention",
    )(qkv, qkv, qkv, _sb_tri())


HALO = 16


def _conv_kernel(p_ref, h_ref, w_ref, b_ref, o_ref):
    d = D_MODEL
    ts = p_ref.shape[1]
    cu = p_ref[0, :, d:2 * d].astype(F32) * p_ref[0, :, 2 * d:].astype(F32)
    cu_prev = h_ref[0, :, d:2 * d].astype(F32) * h_ref[0, :, 2 * d:].astype(F32)
    cu_prev = jnp.where(pl.program_id(1) == 0, 0.0, cu_prev)
    row = lax.broadcasted_iota(jnp.int32, (HALO, d), 0)

    def shifted(k):
        top = jnp.where(row < k, pltpu.roll(cu_prev, k, axis=0), pltpu.roll(cu[:HALO], k, axis=0))
        rest = pltpu.roll(cu, k, axis=0)[HALO:]
        return jnp.concatenate([top, rest], axis=0)

    y = w_ref[0:1, :] * shifted(2) + w_ref[1:2, :] * shifted(1) + w_ref[2:3, :] * cu + b_ref[...]
    o_ref[0] = (p_ref[0, :, :d].astype(F32) * y).astype(o_ref.dtype)


def _conv_gate(proj, conv_w, conv_b):
    b, s, n = proj.shape
    ts = ROW_BLOCK
    return pl.pallas_call(
        _conv_kernel,
        grid=(b, s // ts),
        in_specs=[
            pl.BlockSpec((1, ts, n), lambda bi, i: (bi, i, 0)),
            pl.BlockSpec((1, HALO, n), lambda bi, i: (bi, jnp.maximum(i * (ts // HALO) - 1, 0), 0)),
            pl.BlockSpec((3, D_MODEL), lambda bi, i: (0, 0)),
            pl.BlockSpec((1, D_MODEL), lambda bi, i: (0, 0)),
        ],
        out_specs=pl.BlockSpec((1, ts, D_MODEL), lambda bi, i: (bi, i, 0)),
        out_shape=jax.ShapeDtypeStruct((b, s, D_MODEL), BF16),
        compiler_params=pltpu.CompilerParams(dimension_semantics=("arbitrary", "arbitrary")),
        name="conv_gate",
    )(proj, proj, conv_w, conv_b.reshape(1, D_MODEL))


def _col(x, h):
    lane = lax.broadcasted_iota(jnp.int32, x.shape, 1)
    return jnp.sum(jnp.where(lane == h, x, 0.0), axis=-1, keepdims=True)


def _mlstm_kernel(p_ref, g_ref, bias_ref, o_ref, st_ref, m_ref):
    L = ML_CHUNK
    step = pl.program_id(1)

    @pl.when(step == 0)
    def _():
        st_ref[...] = jnp.zeros_like(st_ref)
        m_ref[...] = jnp.zeros_like(m_ref)

    gates = g_ref[0] + bias_ref[...]
    lane = lax.broadcasted_iota(jnp.int32, gates.shape, 1)
    is_f = (lane >= ML_HEADS) & (lane < 2 * ML_HEADS)
    lf_all = jnp.where(is_f, jnp.minimum(gates, 0.0) - jnp.log1p(jnp.exp(-jnp.abs(gates))), 0.0)

    r = lax.broadcasted_iota(jnp.int32, (L, L), 0)
    c = lax.broadcasted_iota(jnp.int32, (L, L), 1)
    causal = c <= r
    tril = causal.astype(BF16)
    eye = (lax.broadcasted_iota(jnp.int32, (LANES, LANES), 0)
           == lax.broadcasted_iota(jnp.int32, (LANES, LANES), 1)).astype(BF16)
    lane_lo = lax.broadcasted_iota(jnp.int32, (L, LANES), 1) < ML_QK_DIM
    ones_col = (lax.broadcasted_iota(jnp.int32, (L, LANES), 1) == 0).astype(BF16)

    for ck in range(ML_STEP // L):
        rows = slice(ck * L, (ck + 1) * L)
        ig_c = gates[rows]
        lf_c = lf_all[rows]
        b_c = sum(_dot(tril, p) for p in _split3(lf_c))
        lane_c = lax.broadcasted_iota(jnp.int32, (L, LANES), 1)
        src = jnp.where(lane_c >= ML_HEADS, b_c, ig_c)
        src_t = sum(_dot_nt(eye, jnp.concatenate([p, jnp.zeros_like(p)], axis=0))
                    for p in _split3(src))[:, :L]
        m_all = m_ref[...]
        b_last_all = b_c[L - 1:L, :]
        ig_sh = pltpu.roll(ig_c, ML_HEADS, axis=1)
        log_w = b_last_all - b_c + ig_sh
        m_new = jnp.maximum(b_last_all + m_all, jnp.max(log_w, axis=0, keepdims=True))
        w_all = jnp.exp(log_w - m_new)
        decay_all = jnp.exp(b_last_all + m_all - m_new)
        m_inter_all = b_c + m_all

        for h in range(ML_HEADS):
            hp, odd = h // 2, h % 2
            own = lane_lo != bool(odd)
            qh = jnp.where(own, p_ref[0, rows, hp * LANES:(hp + 1) * LANES], 0)
            kh = jnp.where(own, p_ref[0, rows, ML_QK_WIDTH + hp * LANES:ML_QK_WIDTH + (hp + 1) * LANES], 0)
            kh = kh * jnp.asarray(ML_QK_DIM ** -0.5, BF16)
            vh = p_ref[0, rows, 2 * ML_QK_WIDTH + h * LANES:2 * ML_QK_WIDTH + (h + 1) * LANES]
            oh = p_ref[0, rows, 2 * ML_QK_WIDTH + D_MODEL + h * LANES:
                       2 * ML_QK_WIDTH + D_MODEL + (h + 1) * LANES].astype(F32)
            v_ext = jnp.concatenate([vh, ones_col], axis=1)

            b_col = _col(b_c, ML_HEADS + h)
            ig_row = src_t[h:h + 1, :]
            b_row = src_t[ML_HEADS + h:ML_HEADS + h + 1, :]
            log_d = jnp.where(causal, b_col - b_row + ig_row, -jnp.inf)
            m_inter = _col(m_inter_all, ML_HEADS + h)
            m_t = jnp.maximum(m_inter, jnp.max(log_d, axis=-1, keepdims=True))
            d_mat = jnp.exp(log_d - m_t)
            inter = jnp.exp(m_inter - m_t)

            s_mat = _dot_nt(qh, kh) * d_mat
            st = st_ref[h]
            sv = _dot(s_mat.astype(BF16), v_ext)
            qc = _dot(qh, st.astype(BF16))
            tot = sv + inter * qc
            den = tot[:, LANES:LANES + 1]
            h_out = tot[:, :LANES] / jnp.maximum(jnp.abs(den), jnp.exp(-m_t))
            o_ref[0, rows, h * LANES:(h + 1) * LANES] = (
                (1.0 / (1.0 + jnp.exp(-oh))) * h_out).astype(o_ref.dtype)

            w_col = _col(w_all, ML_HEADS + h)
            decay = _col(decay_all, ML_HEADS + h)
            kw = (kh.astype(F32) * w_col).astype(BF16)
            st_ref[h] = decay * st + _dot_tn(kw, v_ext)
        m_ref[...] = m_new


def _mlstm_core(main, gates, bias_row):
    b, s, n = main.shape
    return pl.pallas_call(
        _mlstm_kernel,
        grid=(b, s // ML_STEP),
        in_specs=[
            pl.BlockSpec((1, ML_STEP, n), lambda bi, i: (bi, i, 0)),
            pl.BlockSpec((1, ML_STEP, LANES), lambda bi, i: (bi, i, 0)),
            pl.BlockSpec((1, LANES), lambda bi, i: (0, 0)),
        ],
        out_specs=pl.BlockSpec((1, ML_STEP, D_MODEL), lambda bi, i: (bi, i, 0)),
        out_shape=jax.ShapeDtypeStruct((b, s, D_MODEL), BF16),
        scratch_shapes=[
            pltpu.VMEM((ML_HEADS, LANES, 2 * LANES), F32),
            pltpu.VMEM((1, LANES), F32),
        ],
        compiler_params=pltpu.CompilerParams(dimension_semantics=("arbitrary", "arbitrary")),
        name="mlstm_core",
    )(main, gates, bias_row)


def kernel(x, ln_mix_pre, ln_mix_post, ln_ffn_pre, ln_ffn_post, sb_w_qkv, sb_w_o, gc_w_in, gc_conv_w,
           gc_conv_b, gc_w_out, ml_w_in, ml_b_i, ml_b_f, ml_w_out, ffn_w_gu, ffn_w_down):
    bsz, seq, d = x.shape
    x2d = x.reshape(bsz * seq, d)
    for i in range(DEPTH):
        kind, j = i % N_MIXERS, i // N_MIXERS
        if kind == 0:
            qkv = _norm_proj(x2d, ln_mix_pre[i], sb_w_qkv[j].astype(BF16), BF16)
            mixed = _sb_attention(qkv.reshape(bsz, seq, 3 * d))
            w_out = sb_w_o[j]
        elif kind == 1:
            proj = _norm_proj(x2d, ln_mix_pre[i], gc_w_in[j].astype(BF16), BF16)
            mixed = _conv_gate(proj.reshape(bsz, seq, 3 * d), gc_conv_w[j], gc_conv_b[j])
            w_out = gc_w_out[j]
        else:
            w_in = ml_w_in[j]
            w_gate = jnp.pad(w_in[:, ML_MAIN_WIDTH:], ((0, 0), (0, LANES - 2 * ML_HEADS)))
            main = _norm_proj(x2d, ln_mix_pre[i], w_in[:, :ML_MAIN_WIDTH].astype(BF16), BF16)
            gates = _norm_proj(x2d, ln_mix_pre[i], w_gate.astype(BF16), F32)
            bias_row = jnp.pad(jnp.concatenate([ml_b_i[j], ml_b_f[j]]), (0, LANES - 2 * ML_HEADS))
            mixed = _mlstm_core(main.reshape(bsz, seq, ML_MAIN_WIDTH), gates.reshape(bsz, seq, LANES),
                                bias_row.reshape(1, LANES))
            w_out = ml_w_out[j]
        x2d = _out_ffn(mixed.reshape(bsz * seq, d), x2d, w_out.astype(BF16), ln_mix_post[i],
                       ln_ffn_pre[i], ffn_w_gu[i].astype(BF16), ffn_w_down[i].astype(BF16),
                       ln_ffn_post[i])
    return x2d.reshape(bsz, seq, d)
```

```python
import functools

import jax
import jax.numpy as jnp
from jax import lax
from jax.experimental import pallas as pl
from jax.experimental.pallas import tpu as pltpu

D_MODEL = 1024
DEPTH = 4
N_MIXERS = 3
SB_HEADS = 16
SB_HEAD_DIM = 64
ML_HEADS = 8
ML_QK_DIM = 64
ML_V_DIM = 128
ML_QK_WIDTH = ML_HEADS * ML_QK_DIM
ML_MAIN_WIDTH = 2 * ML_QK_WIDTH + 2 * D_MODEL
ML_CHUNK = 64
D_FF = 2816
RMS_EPS = 1e-6

LANES = 128
ROW_BLOCK = 512
SB_BLOCK = 128
ML_STEP = 2 * ML_CHUNK
FFN_CHUNK = D_FF // 2
VMEM_LIMIT = 56 * 1024 * 1024

F32 = jnp.float32
BF16 = jnp.bfloat16


def _rms(x, g):
    ms = jnp.mean(x * x, axis=-1, keepdims=True)
    return x * lax.rsqrt(ms + RMS_EPS) * g


def _dot(a, b):
    return jnp.dot(a, b, preferred_element_type=F32)


def _dot_nt(a, b):
    return lax.dot_general(a, b, (((1,), (1,)), ((), ())), preferred_element_type=F32)


def _dot_tn(a, b):
    return lax.dot_general(a, b, (((0,), (0,)), ((), ())), preferred_element_type=F32)


def _split3(x):
    p0 = x.astype(BF16)
    r1 = x - p0.astype(F32)
    p1 = r1.astype(BF16)
    p2 = (r1 - p1.astype(F32)).astype(BF16)
    return p0, p1, p2


def _resident(shape):
    nd = len(shape)
    return pl.BlockSpec(shape, lambda *_: (0,) * nd, pipeline_mode=pl.Buffered(1))


def _norm_proj_kernel(x_ref, g_ref, w_ref, o_ref):
    hn = _rms(x_ref[...], g_ref[...]).astype(BF16)
    n = o_ref.shape[-1]
    step = D_MODEL if n % D_MODEL == 0 else n
    for c0 in range(0, n, step):
        o_ref[:, c0:c0 + step] = _dot(hn, w_ref[:, c0:c0 + step]).astype(o_ref.dtype)


def _norm_proj(x2d, g, w, out_dtype):
    m, d = x2d.shape
    n = w.shape[1]
    return pl.pallas_call(
        _norm_proj_kernel,
        grid=(m // ROW_BLOCK,),
        in_specs=[
            pl.BlockSpec((ROW_BLOCK, d), lambda i: (i, 0)),
            _resident((1, d)),
            _resident((d, n)),
        ],
        out_specs=pl.BlockSpec((ROW_BLOCK, n), lambda i: (i, 0)),
        out_shape=jax.ShapeDtypeStruct((m, n), out_dtype),
        compiler_params=pltpu.CompilerParams(
            dimension_semantics=("arbitrary",), vmem_limit_bytes=VMEM_LIMIT),
        name="norm_proj",
    )(x2d, g.reshape(1, d), w)


def _out_ffn_kernel(m_ref, x_ref, wo_ref, gpost_ref, gpre_ref, wgu_ref, wd_ref, gfpost_ref, o_ref):
    mixed = _dot(m_ref[...], wo_ref[...])
    x1 = x_ref[...] + _rms(mixed, gpost_ref[...])
    hn = _rms(x1, gpre_ref[...]).astype(BF16)
    acc = None
    for c0 in range(0, D_FF, FFN_CHUNK):
        g = _dot(hn, wgu_ref[:, c0:c0 + FFN_CHUNK])
        u = _dot(hn, wgu_ref[:, D_FF + c0:D_FF + c0 + FFN_CHUNK])
        a = (g * (1.0 / (1.0 + jnp.exp(-g))) * u).astype(BF16)
        part = _dot(a, wd_ref[c0:c0 + FFN_CHUNK, :])
        acc = part if acc is None else acc + part
    o_ref[...] = x1 + _rms(acc, gfpost_ref[...])


def _out_ffn(mixed2d, x2d, wo, gpost, gpre, wgu, wd, gfpost):
    m, d = x2d.shape
    row = lambda i: (i, 0)
    return pl.pallas_call(
        _out_ffn_kernel,
        grid=(m // ROW_BLOCK,),
        in_specs=[
            pl.BlockSpec((ROW_BLOCK, d), row),
            pl.BlockSpec((ROW_BLOCK, d), row),
            _resident((d, d)),
            _resident((1, d)),
            _resident((1, d)),
            _resident((d, 2 * D_FF)),
            _resident((D_FF, d)),
            _resident((1, d)),
        ],
        out_specs=pl.BlockSpec((ROW_BLOCK, d), row),
        out_shape=jax.ShapeDtypeStruct((m, d), F32),
        compiler_params=pltpu.CompilerParams(
            dimension_semantics=("arbitrary",), vmem_limit_bytes=VMEM_LIMIT),
        name="out_ffn",
    )(mixed2d, x2d, wo, gpost.reshape(1, d), gpre.reshape(1, d), wgu, wd, gfpost.reshape(1, d))


SB_PAIRS = SB_HEADS // 2
SB_SKIP_MASS = 100.0


def _sb_kernel(q_ref, k_ref, v_ref, tri_ref, o_ref, q2_ref, carry_ref, acc_ref):
    t = SB_BLOCK
    qi = pl.program_id(1)
    lane_lo = lax.broadcasted_iota(jnp.int32, (t, LANES), 1) < SB_HEAD_DIM
    row = lax.broadcasted_iota(jnp.int32, (2 * t, t), 0)
    col = lax.broadcasted_iota(jnp.int32, (2 * t, t), 1)
    before = col < jnp.where(row >= t, row - t, row)

    for hp in range(SB_PAIRS):
        qs = q_ref[0, :, hp * LANES:(hp + 1) * LANES] * jnp.asarray(SB_HEAD_DIM ** -0.5, BF16)
        q2_ref[hp] = jnp.concatenate([jnp.where(lane_lo, qs, 0), jnp.where(lane_lo, 0, qs)], axis=0)
    carry_ref[...] = jnp.zeros_like(carry_ref)
    acc_ref[...] = jnp.zeros_like(acc_ref)

    def sweep(j, mask):
        s0 = pl.multiple_of((qi - j) * t, t)
        pairs = range(SB_PAIRS)
        cols = [slice(hp * LANES, (hp + 1) * LANES) for hp in pairs]
        zs = [_dot_nt(q2_ref[hp], k_ref[0, pl.ds(s0, t), cols[hp]]) for hp in pairs]
        sps = []
        for hp in pairs:
            z = zs[hp]
            sp = jnp.maximum(z, 0.0) + jnp.log(1.0 + jnp.exp(-jnp.abs(z)))
            if mask is not None:
                sp = jnp.where(mask, sp, 0.0)
            hi = sp.astype(BF16)
            lo = (sp - hi.astype(F32)).astype(BF16)
            sps.append(jnp.concatenate([hi, lo], axis=1))
        css = [_dot(sps[hp], tri_ref[...]) for hp in pairs]
        a_cats = []
        least = None
        for hp in pairs:
            carry = carry_ref[hp]
            a = jnp.exp(zs[hp] - (css[hp][:, :t] + carry))
            if mask is not None:
                a = jnp.where(mask, a, 0.0)
            a = a.astype(BF16)
            a_cats.append(jnp.concatenate([a[:t], a[t:]], axis=1))
            carry = carry + css[hp][:, t:]
            carry_ref[hp] = carry
            least = carry if least is None else jnp.minimum(least, carry)
        for hp in pairs:
            vt = v_ref[0, pl.ds(s0, t), cols[hp]]
            v_cat = jnp.concatenate([jnp.where(lane_lo, vt, 0), jnp.where(lane_lo, 0, vt)], axis=0)
            acc_ref[hp] += _dot(a_cats[hp], v_cat)
        return jnp.min(least)

    least0 = sweep(0, before)
    lax.while_loop(lambda st: (st[0] <= qi) & (st[1] < SB_SKIP_MASS),
                   lambda st: (st[0] + 1, sweep(st[0], None)),
                   (jnp.int32(1), least0))
    for hp in range(SB_PAIRS):
        o_ref[0, :, hp * LANES:(hp + 1) * LANES] = acc_ref[hp].astype(o_ref.dtype)


def _sb_tri():
    t = SB_BLOCK
    j = lax.broadcasted_iota(jnp.int32, (t, 2 * t), 0)
    s = lax.broadcasted_iota(jnp.int32, (t, 2 * t), 1)
    ext = ((j >= s) | (s >= t)).astype(BF16)
    return jnp.concatenate([ext, ext], axis=0)


def _sb_attention(qkv):
    b, s, _ = qkv.shape
    t = SB_BLOCK
    return pl.pallas_call(
        _sb_kernel,
        grid=(b, s // t),
        in_specs=[
            pl.BlockSpec((1, t, D_MODEL), lambda bi, qi: (bi, qi, 0)),
            pl.BlockSpec((1, s, D_MODEL), lambda bi, qi: (bi, 0, 1)),
            pl.BlockSpec((1, s, D_MODEL), lambda bi, qi: (bi, 0, 2)),
            pl.BlockSpec((2 * t, 2 * t), lambda bi, qi: (0, 0)),
        ],
        out_specs=pl.BlockSpec((1, t, D_MODEL), lambda bi, qi: (bi, qi, 0)),
        out_shape=jax.ShapeDtypeStruct((b, s, D_MODEL), BF16),
        scratch_shapes=[
            pltpu.VMEM((SB_PAIRS, 2 * t, LANES), BF16),
            pltpu.VMEM((SB_PAIRS, 2 * t, t), F32),
            pltpu.VMEM((SB_PAIRS, t, LANES), F32),
        ],
        compiler_params=pltpu.CompilerParams(
            dimension_semantics=("arbitrary", "arbitrary"), vmem_limit_bytes=VMEM_LIMIT),
        name="sb_attention",
    )(qkv, qkv, qkv, _sb_tri())


HALO = 16


def _conv_kernel(p_ref, h_ref, w_ref, b_ref, o_ref):
    d = D_MODEL
    ts = p_ref.shape[1]
    cu = p_ref[0, :, d:2 * d].astype(F32) * p_ref[0, :, 2 * d:].astype(F32)
    cu_prev = h_ref[0, :, d:2 * d].astype(F32) * h_ref[0, :, 2 * d:].astype(F32)
    cu_prev = jnp.where(pl.program_id(1) == 0, 0.0, cu_prev)
    row = lax.broadcasted_iota(jnp.int32, (HALO, d), 0)

    def shifted(k):
        top = jnp.where(row < k, pltpu.roll(cu_prev, k, axis=0), pltpu.roll(cu[:HALO], k, axis=0))
        rest = pltpu.roll(cu, k, axis=0)[HALO:]
        return jnp.concatenate([top, rest], axis=0)

    y = w_ref[0:1, :] * shifted(2) + w_ref[1:2, :] * shifted(1) + w_ref[2:3, :] * cu + b_ref[...]
    o_ref[0] = (p_ref[0, :, :d].astype(F32) * y).astype(o_ref.dtype)


def _conv_gate(proj, conv_w, conv_b):
    b, s, n = proj.shape
    ts = ROW_BLOCK
    return pl.pallas_call(
        _conv_kernel,
        grid=(b, s // ts),
        in_specs=[
            pl.BlockSpec((1, ts, n), lambda bi, i: (bi, i, 0)),
            pl.BlockSpec((1, HALO, n), lambda bi, i: (bi, jnp.maximum(i * (ts // HALO) - 1, 0), 0)),
            pl.BlockSpec((3, D_MODEL), lambda bi, i: (0, 0)),
            pl.BlockSpec((1, D_MODEL), lambda bi, i: (0, 0)),
        ],
        out_specs=pl.BlockSpec((1, ts, D_MODEL), lambda bi, i: (bi, i, 0)),
        out_shape=jax.ShapeDtypeStruct((b, s, D_MODEL), BF16),
        compiler_params=pltpu.CompilerParams(dimension_semantics=("arbitrary", "arbitrary")),
        name="conv_gate",
    )(proj, proj, conv_w, conv_b.reshape(1, D_MODEL))


def _mlstm_expand():
    r = lax.broadcasted_iota(jnp.int32, (LANES, ML_HEADS * LANES), 0)
    c = lax.broadcasted_iota(jnp.int32, (LANES, ML_HEADS * LANES), 1)
    return (r == ML_HEADS + c // LANES).astype(BF16)


def _mlstm_kernel(p_ref, g_ref, bias_ref, e_ref, o_ref, st_ref, m_ref):
    L = ML_CHUNK
    heads = range(ML_HEADS)
    chunks = range(ML_STEP // L)

    @pl.when(pl.program_id(1) == 0)
    def _():
        st_ref[...] = jnp.zeros_like(st_ref)
        m_ref[...] = jnp.zeros_like(m_ref)

    r = lax.broadcasted_iota(jnp.int32, (L, L), 0)
    c = lax.broadcasted_iota(jnp.int32, (L, L), 1)
    causal = c <= r
    tril = causal.astype(BF16)
    eye = (lax.broadcasted_iota(jnp.int32, (LANES, LANES), 0)
           == lax.broadcasted_iota(jnp.int32, (LANES, LANES), 1)).astype(BF16)
    lane = lax.broadcasted_iota(jnp.int32, (L, LANES), 1)
    lane_lo = lane < ML_QK_DIM
    is_f = (lane >= ML_HEADS) & (lane < 2 * ML_HEADS)
    ones_col = (lane == 0).astype(BF16)

    m_all = m_ref[...]
    gate = []
    for ck in chunks:
        rows = slice(ck * L, (ck + 1) * L)
        g = g_ref[0, rows, :] + bias_ref[...]
        lf = jnp.where(is_f, jnp.minimum(g, 0.0) - jnp.log1p(jnp.exp(-jnp.abs(g))), 0.0)
        b_c = sum(_dot(tril, p) for p in _split3(lf))
        src = jnp.where(lane >= ML_HEADS, b_c, g)
        src_t = sum(_dot_nt(eye, jnp.concatenate([p, jnp.zeros_like(p)], axis=0))
                    for p in _split3(src))[:, :L]
        b_last = b_c[L - 1:L, :]
        log_w = b_last - b_c + pltpu.roll(g, ML_HEADS, axis=1)
        m_new = jnp.maximum(b_last + m_all, jnp.max(log_w, axis=0, keepdims=True))
        w_all = jnp.exp(log_w - m_new)
        decay = jnp.exp(b_last + m_all - m_new)
        stacked = jnp.concatenate([b_c, w_all, jnp.broadcast_to(decay, (4, LANES)),
                                   jnp.broadcast_to(m_all, (4, LANES))], axis=0)
        wide = sum(_dot(p, e_ref[...]) for p in _split3(stacked)[:2])
        gate.append((src_t, wide))
        m_all = m_new
    m_ref[...] = m_all

    units = [(ck, h) for ck in chunks for h in heads]
    qs, ks, vs, s_raw = {}, {}, {}, {}
    for ck, h in units:
        rows = slice(ck * L, (ck + 1) * L)
        hp, odd = h // 2, h % 2
        own = lane_lo != bool(odd)
        qs[ck, h] = jnp.where(own, p_ref[0, rows, hp * LANES:(hp + 1) * LANES], 0)
        kh = jnp.where(own, p_ref[0, rows, ML_QK_WIDTH + hp * LANES:ML_QK_WIDTH + (hp + 1) * LANES], 0)
        ks[ck, h] = kh * jnp.asarray(ML_QK_DIM ** -0.5, BF16)
        vh = p_ref[0, rows, 2 * ML_QK_WIDTH + h * LANES:2 * ML_QK_WIDTH + (h + 1) * LANES]
        vs[ck, h] = jnp.concatenate([vh, ones_col], axis=1)
        s_raw[ck, h] = _dot_nt(qs[ck, h], ks[ck, h])
    s_bf, inter, m_ts = {}, {}, {}
    for ck, h in units:
        src_t, wide = gate[ck]
        hb = slice(h * LANES, (h + 1) * LANES)
        b_col = wide[0:L, hb][:, :L]
        m_inter = b_col[:, :1] + wide[2 * L + 4:2 * L + 5, hb][:, :1]
        c_row = src_t[h:h + 1, :] - src_t[ML_HEADS + h:ML_HEADS + h + 1, :]
        log_d = jnp.where(causal, b_col + c_row, -jnp.inf)
        m_t = jnp.maximum(m_inter, jnp.max(log_d, axis=-1, keepdims=True))
        s_bf[ck, h] = (s_raw[ck, h] * jnp.exp(log_d - m_t)).astype(BF16)
        inter[ck, h] = jnp.exp(m_inter - m_t)
        m_ts[ck, h] = m_t
    sv = {u: _dot(s_bf[u], vs[u]) for u in units}

    for ck in chunks:
        rows = slice(ck * L, (ck + 1) * L)
        wide = gate[ck][1]
        qc = [_dot(qs[ck, h], st_ref[h].astype(BF16)) for h in heads]
        kws = []
        for h in heads:
            hb = slice(h * LANES, (h + 1) * LANES)
            tot = sv[ck, h] + inter[ck, h] * qc[h]
            den = tot[:, LANES:LANES + 1]
            h_out = tot[:, :LANES] / jnp.maximum(jnp.abs(den), jnp.exp(-m_ts[ck, h]))
            oh = p_ref[0, rows, 2 * ML_QK_WIDTH + D_MODEL + h * LANES:
                       2 * ML_QK_WIDTH + D_MODEL + (h + 1) * LANES].astype(F32)
            o_ref[0, rows, hb] = ((1.0 / (1.0 + jnp.exp(-oh))) * h_out).astype(o_ref.dtype)
            kws.append((ks[ck, h].astype(F32) * wide[L:2 * L, hb]).astype(BF16))
        upd = [_dot_tn(kws[h], vs[ck, h]) for h in heads]
        for h in heads:
            d_row = wide[2 * L:2 * L + 1, h * LANES:(h + 1) * LANES]
            st_ref[h] = jnp.concatenate([d_row, d_row], axis=1) * st_ref[h] + upd[h]


def _mlstm_core(main, gates, bias_row):
    b, s, n = main.shape
    return pl.pallas_call(
        _mlstm_kernel,
        grid=(b, s // ML_STEP),
        in_specs=[
            pl.BlockSpec((1, ML_STEP, n), lambda bi, i: (bi, i, 0)),
            pl.BlockSpec((1, ML_STEP, LANES), lambda bi, i: (bi, i, 0)),
            pl.BlockSpec((1, LANES), lambda bi, i: (0, 0)),
            pl.BlockSpec((LANES, ML_HEADS * LANES), lambda bi, i: (0, 0)),
        ],
        out_specs=pl.BlockSpec((1, ML_STEP, D_MODEL), lambda bi, i: (bi, i, 0)),
        out_shape=jax.ShapeDtypeStruct((b, s, D_MODEL), BF16),
        scratch_shapes=[
            pltpu.VMEM((ML_HEADS, LANES, 2 * LANES), F32),
            pltpu.VMEM((1, LANES), F32),
        ],
        compiler_params=pltpu.CompilerParams(dimension_semantics=("arbitrary", "arbitrary")),
        name="mlstm_core",
    )(main, gates, bias_row, _mlstm_expand())


def kernel(x, ln_mix_pre, ln_mix_post, ln_ffn_pre, ln_ffn_post, sb_w_qkv, sb_w_o, gc_w_in, gc_conv_w,
           gc_conv_b, gc_w_out, ml_w_in, ml_b_i, ml_b_f, ml_w_out, ffn_w_gu, ffn_w_down):
    bsz, seq, d = x.shape
    x2d = x.reshape(bsz * seq, d)
    for i in range(DEPTH):
        kind, j = i % N_MIXERS, i // N_MIXERS
        if kind == 0:
            qkv = _norm_proj(x2d, ln_mix_pre[i], sb_w_qkv[j].astype(BF16), BF16)
            mixed = _sb_attention(qkv.reshape(bsz, seq, 3 * d))
            w_out = sb_w_o[j]
        elif kind == 1:
            proj = _norm_proj(x2d, ln_mix_pre[i], gc_w_in[j].astype(BF16), BF16)
            mixed = _conv_gate(proj.reshape(bsz, seq, 3 * d), gc_conv_w[j], gc_conv_b[j])
            w_out = gc_w_out[j]
        else:
            w_in = ml_w_in[j]
            w_gate = jnp.pad(w_in[:, ML_MAIN_WIDTH:], ((0, 0), (0, LANES - 2 * ML_HEADS)))
            main = _norm_proj(x2d, ln_mix_pre[i], w_in[:, :ML_MAIN_WIDTH].astype(BF16), BF16)
            gates = _norm_proj(x2d, ln_mix_pre[i], w_gate.astype(BF16), F32)
            bias_row = jnp.pad(jnp.concatenate([ml_b_i[j], ml_b_f[j]]), (0, LANES - 2 * ML_HEADS))
            mixed = _mlstm_core(main.reshape(bsz, seq, ML_MAIN_WIDTH), gates.reshape(bsz, seq, LANES),
                                bias_row.reshape(1, LANES))
            w_out = ml_w_out[j]
        x2d = _out_ffn(mixed.reshape(bsz * seq, d), x2d, w_out.astype(BF16), ln_mix_post[i],
                       ln_ffn_pre[i], ffn_w_gu[i].astype(BF16), ffn_w_down[i].astype(BF16),
                       ln_ffn_post[i])
    return x2d.reshape(bsz, seq, d)
```

```python
import functools

import jax
import jax.numpy as jnp
from jax import lax
from jax.experimental import pallas as pl
from jax.experimental.pallas import tpu as pltpu

D_MODEL = 1024
DEPTH = 4
N_MIXERS = 3
SB_HEADS = 16
SB_HEAD_DIM = 64
ML_HEADS = 8
ML_QK_DIM = 64
ML_V_DIM = 128
ML_QK_WIDTH = ML_HEADS * ML_QK_DIM
ML_MAIN_WIDTH = 2 * ML_QK_WIDTH + 2 * D_MODEL
ML_CHUNK = 64
D_FF = 2816
RMS_EPS = 1e-6

LANES = 128
ROW_BLOCK = 512
SB_BLOCK = 128
ML_STEP = 2 * ML_CHUNK
MXU_TILE = 256
FFN_CHUNKS = ((0, 6 * MXU_TILE), (6 * MXU_TILE, D_FF))
VMEM_LIMIT = 56 * 1024 * 1024

F32 = jnp.float32
BF16 = jnp.bfloat16


def _rms(x, g):
    ms = jnp.mean(x * x, axis=-1, keepdims=True)
    return x * lax.rsqrt(ms + RMS_EPS) * g


def _dot(a, b):
    return jnp.dot(a, b, preferred_element_type=F32)


def _dot_nt(a, b):
    return lax.dot_general(a, b, (((1,), (1,)), ((), ())), preferred_element_type=F32)


def _dot_tn(a, b):
    return lax.dot_general(a, b, (((0,), (0,)), ((), ())), preferred_element_type=F32)


def _split3(x):
    p0 = x.astype(BF16)
    r1 = x - p0.astype(F32)
    p1 = r1.astype(BF16)
    p2 = (r1 - p1.astype(F32)).astype(BF16)
    return p0, p1, p2


def _resident(shape):
    nd = len(shape)
    return pl.BlockSpec(shape, lambda *_: (0,) * nd, pipeline_mode=pl.Buffered(1))


def _resident_layer(stack, j):
    _, r, c = stack.shape
    return pl.BlockSpec((None, r, c), lambda *_: (j, 0, 0), pipeline_mode=pl.Buffered(1))


def _norm_proj_kernel(x_ref, g_ref, w_ref, o_ref):
    hn = _rms(x_ref[...], g_ref[...]).astype(BF16)
    n = o_ref.shape[-1]
    step = D_MODEL if n % D_MODEL == 0 else n
    for c0 in range(0, n, step):
        o_ref[:, c0:c0 + step] = _dot(hn, w_ref[:, c0:c0 + step]).astype(o_ref.dtype)


def _norm_proj(x2d, g, w_stack, j, out_dtype):
    m, d = x2d.shape
    n = w_stack.shape[2]
    return pl.pallas_call(
        _norm_proj_kernel,
        grid=(m // ROW_BLOCK,),
        in_specs=[
            pl.BlockSpec((ROW_BLOCK, d), lambda i: (i, 0)),
            _resident((1, d)),
            _resident_layer(w_stack, j),
        ],
        out_specs=pl.BlockSpec((ROW_BLOCK, n), lambda i: (i, 0)),
        out_shape=jax.ShapeDtypeStruct((m, n), out_dtype),
        compiler_params=pltpu.CompilerParams(
            dimension_semantics=("arbitrary",), vmem_limit_bytes=VMEM_LIMIT),
        name="norm_proj",
    )(x2d, g.reshape(1, d), w_stack)


def _out_ffn_kernel(m_ref, x_ref, wo_ref, gpost_ref, gpre_ref, wgu_ref, wd_ref, gfpost_ref, o_ref):
    mixed = _dot(m_ref[...], wo_ref[...])
    x1 = x_ref[...] + _rms(mixed, gpost_ref[...])
    hn = _rms(x1, gpre_ref[...]).astype(BF16)
    acc = None
    for c0, c1 in FFN_CHUNKS:
        g = _dot(hn, wgu_ref[:, c0:c1])
        u = _dot(hn, wgu_ref[:, D_FF + c0:D_FF + c1])
        a = (g * (1.0 / (1.0 + jnp.exp(-g))) * u).astype(BF16)
        part = _dot(a, wd_ref[c0:c1, :])
        acc = part if acc is None else acc + part
    o_ref[...] = x1 + _rms(acc, gfpost_ref[...])


def _out_ffn(mixed2d, x2d, wo_stack, j, gpost, gpre, wgu_stack, wd_stack, i, gfpost):
    m, d = x2d.shape
    row = lambda i: (i, 0)
    return pl.pallas_call(
        _out_ffn_kernel,
        grid=(m // ROW_BLOCK,),
        in_specs=[
            pl.BlockSpec((ROW_BLOCK, d), row),
            pl.BlockSpec((ROW_BLOCK, d), row),
            _resident_layer(wo_stack, j),
            _resident((1, d)),
            _resident((1, d)),
            _resident_layer(wgu_stack, i),
            _resident_layer(wd_stack, i),
            _resident((1, d)),
        ],
        out_specs=pl.BlockSpec((ROW_BLOCK, d), row),
        out_shape=jax.ShapeDtypeStruct((m, d), F32),
        compiler_params=pltpu.CompilerParams(
            dimension_semantics=("arbitrary",), vmem_limit_bytes=VMEM_LIMIT),
        name="out_ffn",
    )(mixed2d, x2d, wo_stack, gpost.reshape(1, d), gpre.reshape(1, d), wgu_stack, wd_stack,
      gfpost.reshape(1, d))


SB_PAIRS = SB_HEADS // 2
SB_SKIP_MASS = 88.0


def _sb_kernel(q_ref, k_ref, v_ref, tri_ref, o_ref, q2_ref, carry_ref, acc_ref):
    t = SB_BLOCK
    qi = pl.program_id(1)
    lane_lo = lax.broadcasted_iota(jnp.int32, (t, LANES), 1) < SB_HEAD_DIM
    row = lax.broadcasted_iota(jnp.int32, (2 * t, t), 0)
    col = lax.broadcasted_iota(jnp.int32, (2 * t, t), 1)
    before = col < jnp.where(row >= t, row - t, row)

    for hp in range(SB_PAIRS):
        qs = q_ref[0, :, hp * LANES:(hp + 1) * LANES] * jnp.asarray(SB_HEAD_DIM ** -0.5, BF16)
        q2_ref[hp] = jnp.concatenate([jnp.where(lane_lo, qs, 0), jnp.where(lane_lo, 0, qs)], axis=0)
    carry_ref[...] = jnp.zeros_like(carry_ref)
    acc_ref[...] = jnp.zeros_like(acc_ref)

    def sweep(j, mask):
        s0 = pl.multiple_of((qi - j) * t, t)
        pairs = range(SB_PAIRS)
        cols = [slice(hp * LANES, (hp + 1) * LANES) for hp in pairs]
        zs = [_dot_nt(q2_ref[hp], k_ref[0, pl.ds(s0, t), cols[hp]]) for hp in pairs]
        sps = []
        for hp in pairs:
            z = zs[hp]
            sp = jnp.maximum(z, 0.0) + jnp.log(1.0 + jnp.exp(-jnp.abs(z)))
            if mask is not None:
                sp = jnp.where(mask, sp, 0.0)
            hi = sp.astype(BF16)
            lo = (sp - hi.astype(F32)).astype(BF16)
            sps.append(jnp.concatenate([hi, lo], axis=1))
        css = [_dot(sps[hp], tri_ref[...]) for hp in pairs]
        a_cats = []
        least = None
        for hp in pairs:
            carry = carry_ref[hp]
            a = jnp.exp(zs[hp] - (css[hp][:, :t] + carry))
            if mask is not None:
                a = jnp.where(mask, a, 0.0)
            a = a.astype(BF16)
            a_cats.append(jnp.concatenate([a[:t], a[t:]], axis=1))
            carry = carry + css[hp][:, t:]
            carry_ref[hp] = carry
            least = carry if least is None else jnp.minimum(least, carry)
        for hp in pairs:
            vt = v_ref[0, pl.ds(s0, t), cols[hp]]
            v_cat = jnp.concatenate([jnp.where(lane_lo, vt, 0), jnp.where(lane_lo, 0, vt)], axis=0)
            acc_ref[hp] += _dot(a_cats[hp], v_cat)
        return jnp.min(least)

    least0 = sweep(0, before)
    lax.while_loop(lambda st: (st[0] <= qi) & (st[1] < SB_SKIP_MASS),
                   lambda st: (st[0] + 1, sweep(st[0], None)),
                   (jnp.int32(1), least0))
    for hp in range(SB_PAIRS):
        o_ref[0, :, hp * LANES:(hp + 1) * LANES] = acc_ref[hp].astype(o_ref.dtype)


def _sb_tri():
    t = SB_BLOCK
    j = lax.broadcasted_iota(jnp.int32, (t, 2 * t), 0)
    s = lax.broadcasted_iota(jnp.int32, (t, 2 * t), 1)
    ext = ((j >= s) | (s >= t)).astype(BF16)
    return jnp.concatenate([ext, ext], axis=0)


def _sb_attention(qkv):
    b, s, _ = qkv.shape
    t = SB_BLOCK
    return pl.pallas_call(
        _sb_kernel,
        grid=(b, s // t),
        in_specs=[
            pl.BlockSpec((1, t, D_MODEL), lambda bi, qi: (bi, qi, 0)),
            pl.BlockSpec((1, s, D_MODEL), lambda bi, qi: (bi, 0, 1)),
            pl.BlockSpec((1, s, D_MODEL), lambda bi, qi: (bi, 0, 2)),
            pl.BlockSpec((2 * t, 2 * t), lambda bi, qi: (0, 0)),
        ],
        out_specs=pl.BlockSpec((1, t, D_MODEL), lambda bi, qi: (bi, qi, 0)),
        out_shape=jax.ShapeDtypeStruct((b, s, D_MODEL), BF16),
        scratch_shapes=[
            pltpu.VMEM((SB_PAIRS, 2 * t, LANES), BF16),
            pltpu.VMEM((SB_PAIRS, 2 * t, t), F32),
            pltpu.VMEM((SB_PAIRS, t, LANES), F32),
        ],
        compiler_params=pltpu.CompilerParams(
            dimension_semantics=("arbitrary", "arbitrary"), vmem_limit_bytes=VMEM_LIMIT),
        name="sb_attention",
    )(qkv, qkv, qkv, _sb_tri())


HALO = 16


def _conv_kernel(p_ref, h_ref, w_ref, b_ref, o_ref):
    d = D_MODEL
    ts = p_ref.shape[1]
    cu = p_ref[0, :, d:2 * d].astype(F32) * p_ref[0, :, 2 * d:].astype(F32)
    cu_prev = h_ref[0, :, d:2 * d].astype(F32) * h_ref[0, :, 2 * d:].astype(F32)
    cu_prev = jnp.where(pl.program_id(1) == 0, 0.0, cu_prev)
    row = lax.broadcasted_iota(jnp.int32, (HALO, d), 0)

    def shifted(k):
        top = jnp.where(row < k, pltpu.roll(cu_prev, k, axis=0), pltpu.roll(cu[:HALO], k, axis=0))
        rest = pltpu.roll(cu, k, axis=0)[HALO:]
        return jnp.concatenate([top, rest], axis=0)

    y = w_ref[0:1, :] * shifted(2) + w_ref[1:2, :] * shifted(1) + w_ref[2:3, :] * cu + b_ref[...]
    o_ref[0] = (p_ref[0, :, :d].astype(F32) * y).astype(o_ref.dtype)


def _conv_gate(proj, conv_w, conv_b):
    b, s, n = proj.shape
    ts = ROW_BLOCK
    return pl.pallas_call(
        _conv_kernel,
        grid=(b, s // ts),
        in_specs=[
            pl.BlockSpec((1, ts, n), lambda bi, i: (bi, i, 0)),
            pl.BlockSpec((1, HALO, n), lambda bi, i: (bi, jnp.maximum(i * (ts // HALO) - 1, 0), 0)),
            pl.BlockSpec((3, D_MODEL), lambda bi, i: (0, 0)),
            pl.BlockSpec((1, D_MODEL), lambda bi, i: (0, 0)),
        ],
        out_specs=pl.BlockSpec((1, ts, D_MODEL), lambda bi, i: (bi, i, 0)),
        out_shape=jax.ShapeDtypeStruct((b, s, D_MODEL), BF16),
        compiler_params=pltpu.CompilerParams(dimension_semantics=("arbitrary", "arbitrary")),
        name="conv_gate",
    )(proj, proj, conv_w, conv_b.reshape(1, D_MODEL))


def _mlstm_expand():
    r = lax.broadcasted_iota(jnp.int32, (LANES, ML_HEADS * LANES), 0)
    c = lax.broadcasted_iota(jnp.int32, (LANES, ML_HEADS * LANES), 1)
    return (r == ML_HEADS + c // LANES).astype(BF16)


def _mlstm_kernel(p_ref, g_ref, bias_ref, e_ref, o_ref, st_ref, m_ref):
    L = ML_CHUNK
    heads = range(ML_HEADS)
    chunks = range(ML_STEP // L)

    @pl.when(pl.program_id(1) == 0)
    def _():
        st_ref[...] = jnp.zeros_like(st_ref)
        m_ref[...] = jnp.zeros_like(m_ref)

    r = lax.broadcasted_iota(jnp.int32, (L, L), 0)
    c = lax.broadcasted_iota(jnp.int32, (L, L), 1)
    causal = c <= r
    tril = causal.astype(BF16)
    eye = (lax.broadcasted_iota(jnp.int32, (LANES, LANES), 0)
           == lax.broadcasted_iota(jnp.int32, (LANES, LANES), 1)).astype(BF16)
    lane = lax.broadcasted_iota(jnp.int32, (L, LANES), 1)
    lane_lo = lane < ML_QK_DIM
    is_f = (lane >= ML_HEADS) & (lane < 2 * ML_HEADS)
    ones_col = (lane == 0).astype(BF16)

    m_all = m_ref[...]
    gate = []
    for ck in chunks:
        rows = slice(ck * L, (ck + 1) * L)
        g = g_ref[0, rows, :] + bias_ref[...]
        lf = jnp.where(is_f, jnp.minimum(g, 0.0) - jnp.log1p(jnp.exp(-jnp.abs(g))), 0.0)
        b_c = sum(_dot(tril, p) for p in _split3(lf))
        src = jnp.where(lane >= ML_HEADS, b_c, g)
        src_t = sum(_dot_nt(eye, jnp.concatenate([p, jnp.zeros_like(p)], axis=0))
                    for p in _split3(src))[:, :L]
        b_last = b_c[L - 1:L, :]
        log_w = b_last - b_c + pltpu.roll(g, ML_HEADS, axis=1)
        m_new = jnp.maximum(b_last + m_all, jnp.max(log_w, axis=0, keepdims=True))
        w_all = jnp.exp(log_w - m_new)
        decay = jnp.exp(b_last + m_all - m_new)
        stacked = jnp.concatenate([b_c, w_all, jnp.broadcast_to(decay, (4, LANES)),
                                   jnp.broadcast_to(m_all, (4, LANES))], axis=0)
        wide = sum(_dot(p, e_ref[...]) for p in _split3(stacked)[:2])
        gate.append((src_t, wide))
        m_all = m_new
    m_ref[...] = m_all

    units = [(ck, h) for ck in chunks for h in heads]
    qs, ks, vs, s_raw = {}, {}, {}, {}
    for ck, h in units:
        rows = slice(ck * L, (ck + 1) * L)
        hp, odd = h // 2, h % 2
        own = lane_lo != bool(odd)
        qs[ck, h] = jnp.where(own, p_ref[0, rows, hp * LANES:(hp + 1) * LANES], 0)
        kh = jnp.where(own, p_ref[0, rows, ML_QK_WIDTH + hp * LANES:ML_QK_WIDTH + (hp + 1) * LANES], 0)
        ks[ck, h] = kh * jnp.asarray(ML_QK_DIM ** -0.5, BF16)
        vh = p_ref[0, rows, 2 * ML_QK_WIDTH + h * LANES:2 * ML_QK_WIDTH + (h + 1) * LANES]
        vs[ck, h] = jnp.concatenate([vh, ones_col], axis=1)
        s_raw[ck, h] = _dot_nt(qs[ck, h], ks[ck, h])
    s_bf, inter, m_ts = {}, {}, {}
    for ck, h in units:
        src_t, wide = gate[ck]
        hb = slice(h * LANES, (h + 1) * LANES)
        b_col = wide[0:L, hb][:, :L]
        m_inter = b_col[:, :1] + wide[2 * L + 4:2 * L + 5, hb][:, :1]
        c_row = src_t[h:h + 1, :] - src_t[ML_HEADS + h:ML_HEADS + h + 1, :]
        log_d = jnp.where(causal, b_col + c_row, -jnp.inf)
        m_t = jnp.maximum(m_inter, jnp.max(log_d, axis=-1, keepdims=True))
        s_bf[ck, h] = (s_raw[ck, h] * jnp.exp(log_d - m_t)).astype(BF16)
        inter[ck, h] = jnp.exp(m_inter - m_t)
        m_ts[ck, h] = m_t
    sv = {u: _dot(s_bf[u], vs[u]) for u in units}

    for ck in chunks:
        rows = slice(ck * L, (ck + 1) * L)
        wide = gate[ck][1]
        qc = [_dot(qs[ck, h], st_ref[h].astype(BF16)) for h in heads]
        kws = []
        for h in heads:
            hb = slice(h * LANES, (h + 1) * LANES)
            tot = sv[ck, h] + inter[ck, h] * qc[h]
            den = tot[:, LANES:LANES + 1]
            h_out = tot[:, :LANES] / jnp.maximum(jnp.abs(den), jnp.exp(-m_ts[ck, h]))
            oh = p_ref[0, rows, 2 * ML_QK_WIDTH + D_MODEL + h * LANES:
                       2 * ML_QK_WIDTH + D_MODEL + (h + 1) * LANES].astype(F32)
            o_ref[0, rows, hb] = ((1.0 / (1.0 + jnp.exp(-oh))) * h_out).astype(o_ref.dtype)
            kws.append((ks[ck, h].astype(F32) * wide[L:2 * L, hb]).astype(BF16))
        upd = [_dot_tn(kws[h], vs[ck, h]) for h in heads]
        for h in heads:
            d_row = wide[2 * L:2 * L + 1, h * LANES:(h + 1) * LANES]
            st_ref[h] = jnp.concatenate([d_row, d_row], axis=1) * st_ref[h] + upd[h]


def _mlstm_core(main, gates, bias_row):
    b, s, n = main.shape
    return pl.pallas_call(
        _mlstm_kernel,
        grid=(b, s // ML_STEP),
        in_specs=[
            pl.BlockSpec((1, ML_STEP, n), lambda bi, i: (bi, i, 0)),
            pl.BlockSpec((1, ML_STEP, LANES), lambda bi, i: (bi, i, 0)),
            pl.BlockSpec((1, LANES), lambda bi, i: (0, 0)),
            pl.BlockSpec((LANES, ML_HEADS * LANES), lambda bi, i: (0, 0)),
        ],
        out_specs=pl.BlockSpec((1, ML_STEP, D_MODEL), lambda bi, i: (bi, i, 0)),
        out_shape=jax.ShapeDtypeStruct((b, s, D_MODEL), BF16),
        scratch_shapes=[
            pltpu.VMEM((ML_HEADS, LANES, 2 * LANES), F32),
            pltpu.VMEM((1, LANES), F32),
        ],
        compiler_params=pltpu.CompilerParams(dimension_semantics=("arbitrary", "arbitrary")),
        name="mlstm_core",
    )(main, gates, bias_row, _mlstm_expand())


def kernel(x, ln_mix_pre, ln_mix_post, ln_ffn_pre, ln_ffn_post, sb_w_qkv, sb_w_o, gc_w_in, gc_conv_w,
           gc_conv_b, gc_w_out, ml_w_in, ml_b_i, ml_b_f, ml_w_out, ffn_w_gu, ffn_w_down):
    bsz, seq, d = x.shape
    x2d = x.reshape(bsz * seq, d)
    sb_qkv, sb_o = sb_w_qkv.astype(BF16), sb_w_o.astype(BF16)
    gc_in, gc_out = gc_w_in.astype(BF16), gc_w_out.astype(BF16)
    ml_main = ml_w_in[:, :, :ML_MAIN_WIDTH].astype(BF16)
    ml_gate = jnp.pad(ml_w_in[:, :, ML_MAIN_WIDTH:],
                      ((0, 0), (0, 0), (0, LANES - 2 * ML_HEADS))).astype(BF16)
    ml_out = ml_w_out.astype(BF16)
    w_gu, w_down = ffn_w_gu.astype(BF16), ffn_w_down.astype(BF16)
    for i in range(DEPTH):
        kind, j = i % N_MIXERS, i // N_MIXERS
        if kind == 0:
            qkv = _norm_proj(x2d, ln_mix_pre[i], sb_qkv, j, BF16)
            mixed = _sb_attention(qkv.reshape(bsz, seq, 3 * d))
            w_out = sb_o
        elif kind == 1:
            proj = _norm_proj(x2d, ln_mix_pre[i], gc_in, j, BF16)
            mixed = _conv_gate(proj.reshape(bsz, seq, 3 * d), gc_conv_w[j], gc_conv_b[j])
            w_out = gc_out
        else:
            main = _norm_proj(x2d, ln_mix_pre[i], ml_main, j, BF16)
            gates = _norm_proj(x2d, ln_mix_pre[i], ml_gate, j, F32)
            bias_row = jnp.pad(jnp.concatenate([ml_b_i[j], ml_b_f[j]]), (0, LANES - 2 * ML_HEADS))
            mixed = _mlstm_core(main.reshape(bsz, seq, ML_MAIN_WIDTH), gates.reshape(bsz, seq, LANES),
                                bias_row.reshape(1, LANES))
            w_out = ml_out
        x2d = _out_ffn(mixed.reshape(bsz * seq, d), x2d, w_out, j, ln_mix_post[i], ln_ffn_pre[i],
                       w_gu, w_down, i, ln_ffn_post[i])
    return x2d.reshape(bsz, seq, d)
```

```python
import functools

import jax
import jax.numpy as jnp
from jax import lax
from jax.experimental import pallas as pl
from jax.experimental.pallas import tpu as pltpu

D_MODEL = 1024
DEPTH = 4
N_MIXERS = 3
SB_HEADS = 16
SB_HEAD_DIM = 64
ML_HEADS = 8
ML_QK_DIM = 64
ML_V_DIM = 128
ML_QK_WIDTH = ML_HEADS * ML_QK_DIM
ML_MAIN_WIDTH = 2 * ML_QK_WIDTH + 2 * D_MODEL
ML_CHUNK = 64
D_FF = 2816
RMS_EPS = 1e-6

LANES = 128
ROW_BLOCK = 512
SB_BLOCK = 128
ML_STEP = 2 * ML_CHUNK
MXU_TILE = 256
FFN_CHUNKS = ((0, 6 * MXU_TILE), (6 * MXU_TILE, D_FF))
VMEM_LIMIT = 56 * 1024 * 1024

F32 = jnp.float32
BF16 = jnp.bfloat16


def _rms(x, g):
    ms = jnp.mean(x * x, axis=-1, keepdims=True)
    return x * lax.rsqrt(ms + RMS_EPS) * g


def _dot(a, b):
    return jnp.dot(a, b, preferred_element_type=F32)


def _dot_nt(a, b):
    return lax.dot_general(a, b, (((1,), (1,)), ((), ())), preferred_element_type=F32)


def _dot_tn(a, b):
    return lax.dot_general(a, b, (((0,), (0,)), ((), ())), preferred_element_type=F32)


def _split3(x):
    p0 = x.astype(BF16)
    r1 = x - p0.astype(F32)
    p1 = r1.astype(BF16)
    p2 = (r1 - p1.astype(F32)).astype(BF16)
    return p0, p1, p2


def _resident(shape):
    nd = len(shape)
    return pl.BlockSpec(shape, lambda *_: (0,) * nd, pipeline_mode=pl.Buffered(1))


def _resident_layer(stack, j):
    _, r, c = stack.shape
    return pl.BlockSpec((None, r, c), lambda *_: (j, 0, 0), pipeline_mode=pl.Buffered(1))


def _norm_proj_kernel(x_ref, g_ref, w_ref, o_ref):
    hn = _rms(x_ref[...], g_ref[...]).astype(BF16)
    n = o_ref.shape[-1]
    step = D_MODEL if n % D_MODEL == 0 else n
    for c0 in range(0, n, step):
        o_ref[:, c0:c0 + step] = _dot(hn, w_ref[:, c0:c0 + step]).astype(o_ref.dtype)


def _norm_proj(x2d, g, w_stack, j, out_dtype):
    m, d = x2d.shape
    n = w_stack.shape[2]
    return pl.pallas_call(
        _norm_proj_kernel,
        grid=(m // ROW_BLOCK,),
        in_specs=[
            pl.BlockSpec((ROW_BLOCK, d), lambda i: (i, 0)),
            _resident((1, d)),
            _resident_layer(w_stack, j),
        ],
        out_specs=pl.BlockSpec((ROW_BLOCK, n), lambda i: (i, 0)),
        out_shape=jax.ShapeDtypeStruct((m, n), out_dtype),
        compiler_params=pltpu.CompilerParams(
            dimension_semantics=("arbitrary",), vmem_limit_bytes=VMEM_LIMIT),
        name="norm_proj",
    )(x2d, g.reshape(1, d), w_stack)


def _out_ffn_kernel(m_ref, x_ref, wo_ref, gpost_ref, gpre_ref, wgu_ref, wd_ref, gfpost_ref, o_ref):
    rows = o_ref.shape[0]
    halves = [slice(k * rows // 2, (k + 1) * rows // 2) for k in range(2)]
    mixed = [_dot(m_ref[h, :], wo_ref[...]) for h in halves]
    x1 = [x_ref[h, :] + _rms(mx, gpost_ref[...]) for h, mx in zip(halves, mixed)]
    hn = [_rms(v, gpre_ref[...]).astype(BF16) for v in x1]
    acc = [None, None]
    for c0, c1 in FFN_CHUNKS:
        g = [_dot(v, wgu_ref[:, c0:c1]) for v in hn]
        u = [_dot(v, wgu_ref[:, D_FF + c0:D_FF + c1]) for v in hn]
        a = [(gk * (1.0 / (1.0 + jnp.exp(-gk))) * uk).astype(BF16) for gk, uk in zip(g, u)]
        part = [_dot(ak, wd_ref[c0:c1, :]) for ak in a]
        acc = [p if c is None else c + p for c, p in zip(acc, part)]
    for h, v, c in zip(halves, x1, acc):
        o_ref[h, :] = v + _rms(c, gfpost_ref[...])


def _out_ffn(mixed2d, x2d, wo_stack, j, gpost, gpre, wgu_stack, wd_stack, i, gfpost):
    m, d = x2d.shape
    row = lambda i: (i, 0)
    return pl.pallas_call(
        _out_ffn_kernel,
        grid=(m // ROW_BLOCK,),
        in_specs=[
            pl.BlockSpec((ROW_BLOCK, d), row),
            pl.BlockSpec((ROW_BLOCK, d), row),
            _resident_layer(wo_stack, j),
            _resident((1, d)),
            _resident((1, d)),
            _resident_layer(wgu_stack, i),
            _resident_layer(wd_stack, i),
            _resident((1, d)),
        ],
        out_specs=pl.BlockSpec((ROW_BLOCK, d), row),
        out_shape=jax.ShapeDtypeStruct((m, d), F32),
        compiler_params=pltpu.CompilerParams(
            dimension_semantics=("arbitrary",), vmem_limit_bytes=VMEM_LIMIT),
        name="out_ffn",
    )(mixed2d, x2d, wo_stack, gpost.reshape(1, d), gpre.reshape(1, d), wgu_stack, wd_stack,
      gfpost.reshape(1, d))


SB_PAIRS = SB_HEADS // 2
SB_SKIP_MASS = 88.0


def _sb_kernel(q_ref, k_ref, v_ref, tri_ref, o_ref, q2_ref, carry_ref, acc_ref, least_ref):
    t = SB_BLOCK
    qi = pl.program_id(1)
    pairs = range(SB_PAIRS)
    cols = [slice(hp * LANES, (hp + 1) * LANES) for hp in pairs]
    lane_lo = lax.broadcasted_iota(jnp.int32, (t, LANES), 1) < SB_HEAD_DIM
    row = lax.broadcasted_iota(jnp.int32, (2 * t, t), 0)
    col = lax.broadcasted_iota(jnp.int32, (2 * t, t), 1)
    before = col < jnp.where(row >= t, row - t, row)

    for hp in pairs:
        qs = q_ref[0, :, cols[hp]] * jnp.asarray(SB_HEAD_DIM ** -0.5, BF16)
        q2_ref[hp] = jnp.concatenate([jnp.where(lane_lo, qs, 0), jnp.where(lane_lo, 0, qs)], axis=0)

    def sweep(tiles, fresh):
        starts = [pl.multiple_of(ti * t, t) for ti, _ in tiles]
        units = [(hp, k) for hp in pairs for k in range(len(tiles))]
        zs = {(hp, k): _dot_nt(q2_ref[hp], k_ref[0, pl.ds(starts[k], t), cols[hp]]) for hp, k in units}
        sps = {}
        for hp, k in units:
            z = zs[hp, k]
            sp = jnp.maximum(z, 0.0) + jnp.log(1.0 + jnp.exp(-jnp.abs(z)))
            if tiles[k][1] is not None:
                sp = jnp.where(tiles[k][1], sp, 0.0)
            hi = sp.astype(BF16)
            lo = (sp - hi.astype(F32)).astype(BF16)
            sps[hp, k] = jnp.concatenate([hi, lo], axis=1)
        css = {u: _dot(sps[u], tri_ref[...]) for u in units}
        a_cats = []
        least = None
        for hp in pairs:
            carry = None if fresh else carry_ref[hp]
            parts = []
            for k, (_, mask) in enumerate(tiles):
                cs = css[hp, k]
                a = jnp.exp(zs[hp, k] - (cs[:, :t] if carry is None else cs[:, :t] + carry))
                if mask is not None:
                    a = jnp.where(mask, a, 0.0)
                a = a.astype(BF16)
                parts += [a[:t], a[t:]]
                carry = cs[:, t:] if carry is None else carry + cs[:, t:]
            a_cats.append(jnp.concatenate(parts, axis=1))
            carry_ref[hp] = carry
            least = carry if least is None else jnp.minimum(least, carry)
        for hp in pairs:
            vparts = []
            for k in range(len(tiles)):
                vt = v_ref[0, pl.ds(starts[k], t), cols[hp]]
                vparts += [jnp.where(lane_lo, vt, 0), jnp.where(lane_lo, 0, vt)]
            out = _dot(a_cats[hp], jnp.concatenate(vparts, axis=0))
            acc_ref[hp] = out if fresh else acc_ref[hp] + out
        return jnp.min(least)

    @pl.when(qi == 0)
    def _():
        least_ref[0] = sweep([(qi, before)], True)

    @pl.when(qi > 0)
    def _():
        least_ref[0] = sweep([(qi, before), (qi - 1, None)], True)

    lax.while_loop(lambda st: (st[0] <= qi) & (st[1] < SB_SKIP_MASS),
                   lambda st: (st[0] + 1, sweep([(qi - st[0], None)], False)),
                   (jnp.int32(2), least_ref[0]))
    for hp in pairs:
        o_ref[0, :, cols[hp]] = acc_ref[hp].astype(o_ref.dtype)


def _sb_tri():
    t = SB_BLOCK
    j = lax.broadcasted_iota(jnp.int32, (t, 2 * t), 0)
    s = lax.broadcasted_iota(jnp.int32, (t, 2 * t), 1)
    ext = ((j >= s) | (s >= t)).astype(BF16)
    return jnp.concatenate([ext, ext], axis=0)


def _sb_attention(qkv):
    b, s, _ = qkv.shape
    t = SB_BLOCK
    return pl.pallas_call(
        _sb_kernel,
        grid=(b, s // t),
        in_specs=[
            pl.BlockSpec((1, t, D_MODEL), lambda bi, qi: (bi, qi, 0)),
            pl.BlockSpec((1, s, D_MODEL), lambda bi, qi: (bi, 0, 1)),
            pl.BlockSpec((1, s, D_MODEL), lambda bi, qi: (bi, 0, 2)),
            pl.BlockSpec((2 * t, 2 * t), lambda bi, qi: (0, 0)),
        ],
        out_specs=pl.BlockSpec((1, t, D_MODEL), lambda bi, qi: (bi, qi, 0)),
        out_shape=jax.ShapeDtypeStruct((b, s, D_MODEL), BF16),
        scratch_shapes=[
            pltpu.VMEM((SB_PAIRS, 2 * t, LANES), BF16),
            pltpu.VMEM((SB_PAIRS, 2 * t, t), F32),
            pltpu.VMEM((SB_PAIRS, t, LANES), F32),
            pltpu.SMEM((1,), F32),
        ],
        compiler_params=pltpu.CompilerParams(
            dimension_semantics=("arbitrary", "arbitrary"), vmem_limit_bytes=VMEM_LIMIT),
        name="sb_attention",
    )(qkv, qkv, qkv, _sb_tri())


HALO = 16


def _conv_kernel(p_ref, h_ref, w_ref, b_ref, o_ref):
    d = D_MODEL
    ts = p_ref.shape[1]
    cu = p_ref[0, :, d:2 * d].astype(F32) * p_ref[0, :, 2 * d:].astype(F32)
    cu_prev = h_ref[0, :, d:2 * d].astype(F32) * h_ref[0, :, 2 * d:].astype(F32)
    cu_prev = jnp.where(pl.program_id(1) == 0, 0.0, cu_prev)
    row = lax.broadcasted_iota(jnp.int32, (HALO, d), 0)

    def shifted(k):
        top = jnp.where(row < k, pltpu.roll(cu_prev, k, axis=0), pltpu.roll(cu[:HALO], k, axis=0))
        rest = pltpu.roll(cu, k, axis=0)[HALO:]
        return jnp.concatenate([top, rest], axis=0)

    y = w_ref[0:1, :] * shifted(2) + w_ref[1:2, :] * shifted(1) + w_ref[2:3, :] * cu + b_ref[...]
    o_ref[0] = (p_ref[0, :, :d].astype(F32) * y).astype(o_ref.dtype)


def _conv_gate(proj, conv_w, conv_b):
    b, s, n = proj.shape
    ts = ROW_BLOCK
    return pl.pallas_call(
        _conv_kernel,
        grid=(b, s // ts),
        in_specs=[
            pl.BlockSpec((1, ts, n), lambda bi, i: (bi, i, 0)),
            pl.BlockSpec((1, HALO, n), lambda bi, i: (bi, jnp.maximum(i * (ts // HALO) - 1, 0), 0)),
            pl.BlockSpec((3, D_MODEL), lambda bi, i: (0, 0)),
            pl.BlockSpec((1, D_MODEL), lambda bi, i: (0, 0)),
        ],
        out_specs=pl.BlockSpec((1, ts, D_MODEL), lambda bi, i: (bi, i, 0)),
        out_shape=jax.ShapeDtypeStruct((b, s, D_MODEL), BF16),
        compiler_params=pltpu.CompilerParams(dimension_semantics=("arbitrary", "arbitrary")),
        name="conv_gate",
    )(proj, proj, conv_w, conv_b.reshape(1, D_MODEL))


def _mlstm_expand():
    r = lax.broadcasted_iota(jnp.int32, (LANES, ML_HEADS * LANES), 0)
    c = lax.broadcasted_iota(jnp.int32, (LANES, ML_HEADS * LANES), 1)
    return (r == ML_HEADS + c // LANES).astype(BF16)


def _mlstm_kernel(p_ref, g_ref, bias_ref, e_ref, o_ref, st_ref, m_ref):
    L = ML_CHUNK
    heads = range(ML_HEADS)
    chunks = range(ML_STEP // L)

    @pl.when(pl.program_id(1) == 0)
    def _():
        st_ref[...] = jnp.zeros_like(st_ref)
        m_ref[...] = jnp.zeros_like(m_ref)

    r = lax.broadcasted_iota(jnp.int32, (L, L), 0)
    c = lax.broadcasted_iota(jnp.int32, (L, L), 1)
    causal = c <= r
    tril = causal.astype(BF16)
    eye = (lax.broadcasted_iota(jnp.int32, (LANES, LANES), 0)
           == lax.broadcasted_iota(jnp.int32, (LANES, LANES), 1)).astype(BF16)
    lane = lax.broadcasted_iota(jnp.int32, (L, LANES), 1)
    lane_lo = lane < ML_QK_DIM
    is_f = (lane >= ML_HEADS) & (lane < 2 * ML_HEADS)
    ones_col = (lane == 0).astype(BF16)

    m_all = m_ref[...]
    gate = []
    for ck in chunks:
        rows = slice(ck * L, (ck + 1) * L)
        g = g_ref[0, rows, :] + bias_ref[...]
        lf = jnp.where(is_f, jnp.minimum(g, 0.0) - jnp.log1p(jnp.exp(-jnp.abs(g))), 0.0)
        b_c = sum(_dot(tril, p) for p in _split3(lf))
        src = jnp.where(lane >= ML_HEADS, b_c, g)
        src_t = sum(_dot_nt(eye, jnp.concatenate([p, jnp.zeros_like(p)], axis=0))
                    for p in _split3(src))[:, :L]
        b_last = b_c[L - 1:L, :]
        log_w = b_last - b_c + pltpu.roll(g, ML_HEADS, axis=1)
        m_new = jnp.maximum(b_last + m_all, jnp.max(log_w, axis=0, keepdims=True))
        w_all = jnp.exp(log_w - m_new)
        decay = jnp.exp(b_last + m_all - m_new)
        stacked = jnp.concatenate([b_c, w_all, jnp.broadcast_to(decay, (4, LANES)),
                                   jnp.broadcast_to(m_all, (4, LANES))], axis=0)
        wide = sum(_dot(p, e_ref[...]) for p in _split3(stacked)[:2])
        gate.append((src_t, wide))
        m_all = m_new
    m_ref[...] = m_all

    units = [(ck, h) for ck in chunks for h in heads]
    qs, ks, vs, s_raw = {}, {}, {}, {}
    for ck, h in units:
        rows = slice(ck * L, (ck + 1) * L)
        hp, odd = h // 2, h % 2
        own = lane_lo != bool(odd)
        qs[ck, h] = jnp.where(own, p_ref[0, rows, hp * LANES:(hp + 1) * LANES], 0)
        kh = jnp.where(own, p_ref[0, rows, ML_QK_WIDTH + hp * LANES:ML_QK_WIDTH + (hp + 1) * LANES], 0)
        ks[ck, h] = kh * jnp.asarray(ML_QK_DIM ** -0.5, BF16)
        vh = p_ref[0, rows, 2 * ML_QK_WIDTH + h * LANES:2 * ML_QK_WIDTH + (h + 1) * LANES]
        vs[ck, h] = jnp.concatenate([vh, ones_col], axis=1)
        s_raw[ck, h] = _dot_nt(qs[ck, h], ks[ck, h])
    s_bf, inter, m_ts = {}, {}, {}
    for ck, h in units:
        src_t, wide = gate[ck]
        hb = slice(h * LANES, (h + 1) * LANES)
        b_col = wide[0:L, hb][:, :L]
        m_inter = b_col[:, :1] + wide[2 * L + 4:2 * L + 5, hb][:, :1]
        c_row = src_t[h:h + 1, :] - src_t[ML_HEADS + h:ML_HEADS + h + 1, :]
        log_d = jnp.where(causal, b_col + c_row, -jnp.inf)
        m_t = jnp.maximum(m_inter, jnp.max(log_d, axis=-1, keepdims=True))
        s_bf[ck, h] = (s_raw[ck, h] * jnp.exp(log_d - m_t)).astype(BF16)
        inter[ck, h] = jnp.exp(m_inter - m_t)
        m_ts[ck, h] = m_t
    sv = {u: _dot(s_bf[u], vs[u]) for u in units}

    for ck in chunks:
        rows = slice(ck * L, (ck + 1) * L)
        wide = gate[ck][1]
        qc = [_dot(qs[ck, h], st_ref[h].astype(BF16)) for h in heads]
        kws = []
        for h in heads:
            hb = slice(h * LANES, (h + 1) * LANES)
            tot = sv[ck, h] + inter[ck, h] * qc[h]
            den = tot[:, LANES:LANES + 1]
            h_out = tot[:, :LANES] / jnp.maximum(jnp.abs(den), jnp.exp(-m_ts[ck, h]))
            oh = p_ref[0, rows, 2 * ML_QK_WIDTH + D_MODEL + h * LANES:
                       2 * ML_QK_WIDTH + D_MODEL + (h + 1) * LANES].astype(F32)
            o_ref[0, rows, hb] = ((1.0 / (1.0 + jnp.exp(-oh))) * h_out).astype(o_ref.dtype)
            kws.append((ks[ck, h].astype(F32) * wide[L:2 * L, hb]).astype(BF16))
        upd = [_dot_tn(kws[h], vs[ck, h]) for h in heads]
        for h in heads:
            d_row = wide[2 * L:2 * L + 1, h * LANES:(h + 1) * LANES]
            st_ref[h] = jnp.concatenate([d_row, d_row], axis=1) * st_ref[h] + upd[h]


def _mlstm_core(main, gates, bias_row):
    b, s, n = main.shape
    return pl.pallas_call(
        _mlstm_kernel,
        grid=(b, s // ML_STEP),
        in_specs=[
            pl.BlockSpec((1, ML_STEP, n), lambda bi, i: (bi, i, 0)),
            pl.BlockSpec((1, ML_STEP, LANES), lambda bi, i: (bi, i, 0)),
            pl.BlockSpec((1, LANES), lambda bi, i: (0, 0)),
            pl.BlockSpec((LANES, ML_HEADS * LANES), lambda bi, i: (0, 0)),
        ],
        out_specs=pl.BlockSpec((1, ML_STEP, D_MODEL), lambda bi, i: (bi, i, 0)),
        out_shape=jax.ShapeDtypeStruct((b, s, D_MODEL), BF16),
        scratch_shapes=[
            pltpu.VMEM((ML_HEADS, LANES, 2 * LANES), F32),
            pltpu.VMEM((1, LANES), F32),
        ],
        compiler_params=pltpu.CompilerParams(dimension_semantics=("arbitrary", "arbitrary")),
        name="mlstm_core",
    )(main, gates, bias_row, _mlstm_expand())


def kernel(x, ln_mix_pre, ln_mix_post, ln_ffn_pre, ln_ffn_post, sb_w_qkv, sb_w_o, gc_w_in, gc_conv_w,
           gc_conv_b, gc_w_out, ml_w_in, ml_b_i, ml_b_f, ml_w_out, ffn_w_gu, ffn_w_down):
    bsz, seq, d = x.shape
    x2d = x.reshape(bsz * seq, d)
    sb_qkv, sb_o = sb_w_qkv.astype(BF16), sb_w_o.astype(BF16)
    gc_in, gc_out = gc_w_in.astype(BF16), gc_w_out.astype(BF16)
    ml_main = ml_w_in[:, :, :ML_MAIN_WIDTH].astype(BF16)
    ml_gate = jnp.pad(ml_w_in[:, :, ML_MAIN_WIDTH:],
                      ((0, 0), (0, 0), (0, LANES - 2 * ML_HEADS))).astype(BF16)
    ml_out = ml_w_out.astype(BF16)
    w_gu, w_down = ffn_w_gu.astype(BF16), ffn_w_down.astype(BF16)
    for i in range(DEPTH):
        kind, j = i % N_MIXERS, i // N_MIXERS
        if kind == 0:
            qkv = _norm_proj(x2d, ln_mix_pre[i], sb_qkv, j, BF16)
            mixed = _sb_attention(qkv.reshape(bsz, seq, 3 * d))
            w_out = sb_o
        elif kind == 1:
            proj = _norm_proj(x2d, ln_mix_pre[i], gc_in, j, BF16)
            mixed = _conv_gate(proj.reshape(bsz, seq, 3 * d), gc_conv_w[j], gc_conv_b[j])
            w_out = gc_out
        else:
            main = _norm_proj(x2d, ln_mix_pre[i], ml_main, j, BF16)
            gates = _norm_proj(x2d, ln_mix_pre[i], ml_gate, j, F32)
            bias_row = jnp.pad(jnp.concatenate([ml_b_i[j], ml_b_f[j]]), (0, LANES - 2 * ML_HEADS))
            mixed = _mlstm_core(main.reshape(bsz, seq, ML_MAIN_WIDTH), gates.reshape(bsz, seq, LANES),
                                bias_row.reshape(1, LANES))
            w_out = ml_out
        x2d = _out_ffn(mixed.reshape(bsz * seq, d), x2d, w_out, j, ln_mix_post[i], ln_ffn_pre[i],
                       w_gu, w_down, i, ln_ffn_post[i])
    return x2d.reshape(bsz, seq, d)
```

```python
import functools

import jax
import jax.numpy as jnp
from jax import lax
from jax.experimental import pallas as pl
from jax.experimental.pallas import tpu as pltpu

D_MODEL = 1024
DEPTH = 4
N_MIXERS = 3
SB_HEADS = 16
SB_HEAD_DIM = 64
ML_HEADS = 8
ML_QK_DIM = 64
ML_V_DIM = 128
ML_QK_WIDTH = ML_HEADS * ML_QK_DIM
ML_MAIN_WIDTH = 2 * ML_QK_WIDTH + 2 * D_MODEL
ML_CHUNK = 64
D_FF = 2816
RMS_EPS = 1e-6

LANES = 128
ROW_BLOCK = 512
SB_BLOCK = 128
ML_STEP = 2 * ML_CHUNK
MXU_TILE = 256
FFN_CHUNKS = ((0, 6 * MXU_TILE), (6 * MXU_TILE, D_FF))
VMEM_LIMIT = 56 * 1024 * 1024

F32 = jnp.float32
BF16 = jnp.bfloat16


def _rms(x, g):
    ms = jnp.mean(x * x, axis=-1, keepdims=True)
    return x * lax.rsqrt(ms + RMS_EPS) * g


def _dot(a, b):
    return jnp.dot(a, b, preferred_element_type=F32)


def _dot_nt(a, b):
    return lax.dot_general(a, b, (((1,), (1,)), ((), ())), preferred_element_type=F32)


def _dot_tn(a, b):
    return lax.dot_general(a, b, (((0,), (0,)), ((), ())), preferred_element_type=F32)


def _split3(x):
    p0 = x.astype(BF16)
    r1 = x - p0.astype(F32)
    p1 = r1.astype(BF16)
    p2 = (r1 - p1.astype(F32)).astype(BF16)
    return p0, p1, p2


def _resident(shape):
    nd = len(shape)
    return pl.BlockSpec(shape, lambda *_: (0,) * nd, pipeline_mode=pl.Buffered(1))


def _resident_layer(stack, j):
    _, r, c = stack.shape
    return pl.BlockSpec((None, r, c), lambda *_: (j, 0, 0), pipeline_mode=pl.Buffered(1))


def _norm_proj_kernel(x_ref, g_ref, w_ref, o_ref):
    hn = _rms(x_ref[...], g_ref[...]).astype(BF16)
    n = o_ref.shape[-1]
    step = D_MODEL if n % D_MODEL == 0 else n
    for c0 in range(0, n, step):
        o_ref[:, c0:c0 + step] = _dot(hn, w_ref[:, c0:c0 + step]).astype(o_ref.dtype)


def _norm_proj(x2d, g, w_stack, j, out_dtype):
    m, d = x2d.shape
    n = w_stack.shape[2]
    return pl.pallas_call(
        _norm_proj_kernel,
        grid=(m // ROW_BLOCK,),
        in_specs=[
            pl.BlockSpec((ROW_BLOCK, d), lambda i: (i, 0)),
            _resident((1, d)),
            _resident_layer(w_stack, j),
        ],
        out_specs=pl.BlockSpec((ROW_BLOCK, n), lambda i: (i, 0)),
        out_shape=jax.ShapeDtypeStruct((m, n), out_dtype),
        compiler_params=pltpu.CompilerParams(
            dimension_semantics=("arbitrary",), vmem_limit_bytes=VMEM_LIMIT),
        name="norm_proj",
    )(x2d, g.reshape(1, d), w_stack)


def _row_halves(rows):
    return [slice(k * rows // 2, (k + 1) * rows // 2) for k in range(2)]


def _ffn_tail(mixer_out, x_ref, wo_ref, gpost_ref, gpre_ref, wgu_ref, wd_ref, gfpost_ref, o_ref):
    halves = _row_halves(o_ref.shape[0])
    mixed = [_dot(mo, wo_ref[...]) for mo in mixer_out]
    x1 = [x_ref[h, :] + _rms(mx, gpost_ref[...]) for h, mx in zip(halves, mixed)]
    hn = [_rms(v, gpre_ref[...]).astype(BF16) for v in x1]
    acc = [None, None]
    for c0, c1 in FFN_CHUNKS:
        g = [_dot(v, wgu_ref[:, c0:c1]) for v in hn]
        u = [_dot(v, wgu_ref[:, D_FF + c0:D_FF + c1]) for v in hn]
        a = [(gk * (1.0 / (1.0 + jnp.exp(-gk))) * uk).astype(BF16) for gk, uk in zip(g, u)]
        part = [_dot(ak, wd_ref[c0:c1, :]) for ak in a]
        acc = [p if c is None else c + p for c, p in zip(acc, part)]
    for h, v, c in zip(halves, x1, acc):
        o_ref[h, :] = v + _rms(c, gfpost_ref[...])


def _out_ffn_kernel(m_ref, *refs):
    _ffn_tail([m_ref[h, :] for h in _row_halves(m_ref.shape[0])], *refs)


HALO = 16


def _conv_out_ffn_kernel(p_ref, h_ref, cw_ref, cb_ref, *refs, blocks_per_seq):
    d = D_MODEL
    rows = p_ref.shape[0]
    cu = p_ref[:, d:2 * d].astype(F32) * p_ref[:, 2 * d:].astype(F32)
    cu_prev = h_ref[:, d:2 * d].astype(F32) * h_ref[:, 2 * d:].astype(F32)
    cu_prev = jnp.where(pl.program_id(0) % blocks_per_seq == 0, 0.0, cu_prev)
    top_row = lax.broadcasted_iota(jnp.int32, (HALO, d), 0)

    def shifted(k):
        top = jnp.where(top_row < k, pltpu.roll(cu_prev, k, axis=0), pltpu.roll(cu[:HALO], k, axis=0))
        return jnp.concatenate([top, pltpu.roll(cu, k, axis=0)[HALO:]], axis=0)

    y = cw_ref[0:1, :] * shifted(2) + cw_ref[1:2, :] * shifted(1) + cw_ref[2:3, :] * cu + cb_ref[...]
    gated = (p_ref[:, :d].astype(F32) * y).astype(BF16)
    _ffn_tail([gated[h] for h in _row_halves(rows)], *refs)


def _out_ffn(mixer_in, x2d, wo_stack, j, gpost, gpre, wgu_stack, wd_stack, i, gfpost, conv=None):
    m, d = x2d.shape
    row = lambda r: (r, 0)
    if conv is None:
        body, lead_specs, lead_args = _out_ffn_kernel, [pl.BlockSpec((ROW_BLOCK, d), row)], (mixer_in,)
    else:
        conv_w, conv_b, seq = conv
        body = functools.partial(_conv_out_ffn_kernel, blocks_per_seq=seq // ROW_BLOCK)
        lead_specs = [
            pl.BlockSpec((ROW_BLOCK, 3 * d), row),
            pl.BlockSpec((HALO, 3 * d), lambda r: (jnp.maximum(r * (ROW_BLOCK // HALO) - 1, 0), 0)),
            _resident((3, d)),
            _resident((1, d)),
        ]
        lead_args = (mixer_in, mixer_in, conv_w, conv_b.reshape(1, d))
    return pl.pallas_call(
        body,
        grid=(m // ROW_BLOCK,),
        in_specs=lead_specs + [
            pl.BlockSpec((ROW_BLOCK, d), row),
            _resident_layer(wo_stack, j),
            _resident((1, d)),
            _resident((1, d)),
            _resident_layer(wgu_stack, i),
            _resident_layer(wd_stack, i),
            _resident((1, d)),
        ],
        out_specs=pl.BlockSpec((ROW_BLOCK, d), row),
        out_shape=jax.ShapeDtypeStruct((m, d), F32),
        compiler_params=pltpu.CompilerParams(
            dimension_semantics=("arbitrary",), vmem_limit_bytes=VMEM_LIMIT),
        name="out_ffn",
    )(*lead_args, x2d, wo_stack, gpost.reshape(1, d), gpre.reshape(1, d), wgu_stack, wd_stack,
      gfpost.reshape(1, d))


SB_PAIRS = SB_HEADS // 2
SB_SKIP_MASS = 88.0


def _sb_kernel(q_ref, k_ref, v_ref, tri_ref, o_ref, q2_ref, carry_ref, acc_ref, least_ref):
    t = SB_BLOCK
    qi = pl.program_id(1)
    pairs = range(SB_PAIRS)
    cols = [slice(hp * LANES, (hp + 1) * LANES) for hp in pairs]
    lane_lo = lax.broadcasted_iota(jnp.int32, (t, LANES), 1) < SB_HEAD_DIM
    row = lax.broadcasted_iota(jnp.int32, (2 * t, t), 0)
    col = lax.broadcasted_iota(jnp.int32, (2 * t, t), 1)
    before = col < jnp.where(row >= t, row - t, row)

    for hp in pairs:
        qs = q_ref[0, :, cols[hp]] * jnp.asarray(SB_HEAD_DIM ** -0.5, BF16)
        q2_ref[hp] = jnp.concatenate([jnp.where(lane_lo, qs, 0), jnp.where(lane_lo, 0, qs)], axis=0)

    def sweep(tiles, fresh):
        starts = [pl.multiple_of(ti * t, t) for ti, _ in tiles]
        units = [(hp, k) for hp in pairs for k in range(len(tiles))]
        zs = {(hp, k): _dot_nt(q2_ref[hp], k_ref[0, pl.ds(starts[k], t), cols[hp]]) for hp, k in units}
        sps = {}
        for hp, k in units:
            z = zs[hp, k]
            sp = jnp.maximum(z, 0.0) + jnp.log(1.0 + jnp.exp(-jnp.abs(z)))
            if tiles[k][1] is not None:
                sp = jnp.where(tiles[k][1], sp, 0.0)
            sps[hp, k] = sp.astype(BF16)
        css = {u: _dot(sps[u], tri_ref[...]) for u in units}
        a_cats = []
        least = None
        for hp in pairs:
            carry = None if fresh else carry_ref[hp]
            parts = []
            for k, (_, mask) in enumerate(tiles):
                cs = css[hp, k]
                a = jnp.exp(zs[hp, k] - (cs[:, :t] if carry is None else cs[:, :t] + carry))
                if mask is not None:
                    a = jnp.where(mask, a, 0.0)
                a = a.astype(BF16)
                parts += [a[:t], a[t:]]
                carry = cs[:, t:] if carry is None else carry + cs[:, t:]
            a_cats.append(jnp.concatenate(parts, axis=1))
            carry_ref[hp] = carry
            least = carry if least is None else jnp.minimum(least, carry)
        for hp in pairs:
            vparts = []
            for k in range(len(tiles)):
                vt = v_ref[0, pl.ds(starts[k], t), cols[hp]]
                vparts += [jnp.where(lane_lo, vt, 0), jnp.where(lane_lo, 0, vt)]
            out = _dot(a_cats[hp], jnp.concatenate(vparts, axis=0))
            acc_ref[hp] = out if fresh else acc_ref[hp] + out
        return jnp.min(least)

    @pl.when(qi == 0)
    def _():
        least_ref[0] = sweep([(qi, before)], True)

    @pl.when(qi > 0)
    def _():
        least_ref[0] = sweep([(qi, before), (qi - 1, None)], True)

    lax.while_loop(lambda st: (st[0] <= qi) & (st[1] < SB_SKIP_MASS),
                   lambda st: (st[0] + 1, sweep([(qi - st[0], None)], False)),
                   (jnp.int32(2), least_ref[0]))
    for hp in pairs:
        o_ref[0, :, cols[hp]] = acc_ref[hp].astype(o_ref.dtype)


def _sb_tri():
    t = SB_BLOCK
    j = lax.broadcasted_iota(jnp.int32, (t, 2 * t), 0)
    s = lax.broadcasted_iota(jnp.int32, (t, 2 * t), 1)
    return ((j >= s) | (s >= t)).astype(BF16)


def _sb_attention(qkv):
    b, s, _ = qkv.shape
    t = SB_BLOCK
    return pl.pallas_call(
        _sb_kernel,
        grid=(b, s // t),
        in_specs=[
            pl.BlockSpec((1, t, D_MODEL), lambda bi, qi: (bi, qi, 0)),
            pl.BlockSpec((1, s, D_MODEL), lambda bi, qi: (bi, 0, 1)),
            pl.BlockSpec((1, s, D_MODEL), lambda bi, qi: (bi, 0, 2)),
            pl.BlockSpec((t, 2 * t), lambda bi, qi: (0, 0)),
        ],
        out_specs=pl.BlockSpec((1, t, D_MODEL), lambda bi, qi: (bi, qi, 0)),
        out_shape=jax.ShapeDtypeStruct((b, s, D_MODEL), BF16),
        scratch_shapes=[
            pltpu.VMEM((SB_PAIRS, 2 * t, LANES), BF16),
            pltpu.VMEM((SB_PAIRS, 2 * t, t), F32),
            pltpu.VMEM((SB_PAIRS, t, LANES), F32),
            pltpu.SMEM((1,), F32),
        ],
        compiler_params=pltpu.CompilerParams(
            dimension_semantics=("arbitrary", "arbitrary"), vmem_limit_bytes=VMEM_LIMIT),
        name="sb_attention",
    )(qkv, qkv, qkv, _sb_tri())


def _mlstm_expand():
    r = lax.broadcasted_iota(jnp.int32, (LANES, ML_HEADS * LANES), 0)
    c = lax.broadcasted_iota(jnp.int32, (LANES, ML_HEADS * LANES), 1)
    return (r == ML_HEADS + c // LANES).astype(BF16)


def _mlstm_kernel(p_ref, g_ref, bias_ref, e_ref, o_ref, st_ref, m_ref):
    L = ML_CHUNK
    heads = range(ML_HEADS)
    chunks = range(ML_STEP // L)

    @pl.when(pl.program_id(1) == 0)
    def _():
        st_ref[...] = jnp.zeros_like(st_ref)
        m_ref[...] = jnp.zeros_like(m_ref)

    r = lax.broadcasted_iota(jnp.int32, (L, L), 0)
    c = lax.broadcasted_iota(jnp.int32, (L, L), 1)
    causal = c <= r
    tril = causal.astype(BF16)
    eye = (lax.broadcasted_iota(jnp.int32, (LANES, LANES), 0)
           == lax.broadcasted_iota(jnp.int32, (LANES, LANES), 1)).astype(BF16)
    lane = lax.broadcasted_iota(jnp.int32, (L, LANES), 1)
    lane_lo = lane < ML_QK_DIM
    is_f = (lane >= ML_HEADS) & (lane < 2 * ML_HEADS)
    ones_col = (lane == 0).astype(BF16)

    m_all = m_ref[...]
    gate = []
    for ck in chunks:
        rows = slice(ck * L, (ck + 1) * L)
        g = g_ref[0, rows, :] + bias_ref[...]
        lf = jnp.where(is_f, jnp.minimum(g, 0.0) - jnp.log1p(jnp.exp(-jnp.abs(g))), 0.0)
        b_c = sum(_dot(tril, p) for p in _split3(lf))
        src = jnp.where(lane >= ML_HEADS, b_c, g)
        src_t = sum(_dot_nt(eye, jnp.concatenate([p, jnp.zeros_like(p)], axis=0))
                    for p in _split3(src))[:, :L]
        b_last = b_c[L - 1:L, :]
        log_w = b_last - b_c + pltpu.roll(g, ML_HEADS, axis=1)
        m_new = jnp.maximum(b_last + m_all, jnp.max(log_w, axis=0, keepdims=True))
        w_all = jnp.exp(log_w - m_new)
        decay = jnp.exp(b_last + m_all - m_new)
        stacked = jnp.concatenate([b_c, w_all, jnp.broadcast_to(decay, (4, LANES)),
                                   jnp.broadcast_to(m_all, (4, LANES))], axis=0)
        wide = sum(_dot(p, e_ref[...]) for p in _split3(stacked)[:2])
        gate.append((src_t, wide))
        m_all = m_new
    m_ref[...] = m_all

    units = [(ck, h) for ck in chunks for h in heads]
    qs, ks, vs, s_raw = {}, {}, {}, {}
    for ck, h in units:
        rows = slice(ck * L, (ck + 1) * L)
        hp, odd = h // 2, h % 2
        own = lane_lo != bool(odd)
        qs[ck, h] = jnp.where(own, p_ref[0, rows, hp * LANES:(hp + 1) * LANES], 0)
        kh = jnp.where(own, p_ref[0, rows, ML_QK_WIDTH + hp * LANES:ML_QK_WIDTH + (hp + 1) * LANES], 0)
        ks[ck, h] = kh * jnp.asarray(ML_QK_DIM ** -0.5, BF16)
        vh = p_ref[0, rows, 2 * ML_QK_WIDTH + h * LANES:2 * ML_QK_WIDTH + (h + 1) * LANES]
        vs[ck, h] = jnp.concatenate([vh, ones_col], axis=1)
        s_raw[ck, h] = _dot_nt(qs[ck, h], ks[ck, h])
    s_bf, inter, m_ts = {}, {}, {}
    for ck, h in units:
        src_t, wide = gate[ck]
        hb = slice(h * LANES, (h + 1) * LANES)
        b_col = wide[0:L, hb][:, :L]
        m_inter = b_col[:, :1] + wide[2 * L + 4:2 * L + 5, hb][:, :1]
        c_row = src_t[h:h + 1, :] - src_t[ML_HEADS + h:ML_HEADS + h + 1, :]
        log_d = jnp.where(causal, b_col + c_row, -jnp.inf)
        m_t = jnp.maximum(m_inter, jnp.max(log_d, axis=-1, keepdims=True))
        s_bf[ck, h] = (s_raw[ck, h] * jnp.exp(log_d - m_t)).astype(BF16)
        inter[ck, h] = jnp.exp(m_inter - m_t)
        m_ts[ck, h] = m_t
    sv = {u: _dot(s_bf[u], vs[u]) for u in units}

    for ck in chunks:
        rows = slice(ck * L, (ck + 1) * L)
        wide = gate[ck][1]
        qc = [_dot(qs[ck, h], st_ref[h].astype(BF16)) for h in heads]
        kws = []
        for h in heads:
            hb = slice(h * LANES, (h + 1) * LANES)
            tot = sv[ck, h] + inter[ck, h] * qc[h]
            den = tot[:, LANES:LANES + 1]
            h_out = tot[:, :LANES] / jnp.maximum(jnp.abs(den), jnp.exp(-m_ts[ck, h]))
            oh = p_ref[0, rows, 2 * ML_QK_WIDTH + D_MODEL + h * LANES:
                       2 * ML_QK_WIDTH + D_MODEL + (h + 1) * LANES].astype(F32)
            o_ref[0, rows, hb] = ((1.0 / (1.0 + jnp.exp(-oh))) * h_out).astype(o_ref.dtype)
            kws.append((ks[ck, h].astype(F32) * wide[L:2 * L, hb]).astype(BF16))
        upd = [_dot_tn(kws[h], vs[ck, h]) for h in heads]
        for h in heads:
            d_row = wide[2 * L:2 * L + 1, h * LANES:(h + 1) * LANES]
            st_ref[h] = jnp.concatenate([d_row, d_row], axis=1) * st_ref[h] + upd[h]


def _mlstm_core(main, gates, bias_row):
    b, s, n = main.shape
    return pl.pallas_call(
        _mlstm_kernel,
        grid=(b, s // ML_STEP),
        in_specs=[
            pl.BlockSpec((1, ML_STEP, n), lambda bi, i: (bi, i, 0)),
            pl.BlockSpec((1, ML_STEP, LANES), lambda bi, i: (bi, i, 0)),
            pl.BlockSpec((1, LANES), lambda bi, i: (0, 0)),
            pl.BlockSpec((LANES, ML_HEADS * LANES), lambda bi, i: (0, 0)),
        ],
        out_specs=pl.BlockSpec((1, ML_STEP, D_MODEL), lambda bi, i: (bi, i, 0)),
        out_shape=jax.ShapeDtypeStruct((b, s, D_MODEL), BF16),
        scratch_shapes=[
            pltpu.VMEM((ML_HEADS, LANES, 2 * LANES), F32),
            pltpu.VMEM((1, LANES), F32),
        ],
        compiler_params=pltpu.CompilerParams(dimension_semantics=("arbitrary", "arbitrary")),
        name="mlstm_core",
    )(main, gates, bias_row, _mlstm_expand())


def kernel(x, ln_mix_pre, ln_mix_post, ln_ffn_pre, ln_ffn_post, sb_w_qkv, sb_w_o, gc_w_in, gc_conv_w,
           gc_conv_b, gc_w_out, ml_w_in, ml_b_i, ml_b_f, ml_w_out, ffn_w_gu, ffn_w_down):
    bsz, seq, d = x.shape
    x2d = x.reshape(bsz * seq, d)
    sb_qkv, sb_o = sb_w_qkv.astype(BF16), sb_w_o.astype(BF16)
    gc_in, gc_out = gc_w_in.astype(BF16), gc_w_out.astype(BF16)
    ml_main = ml_w_in[:, :, :ML_MAIN_WIDTH].astype(BF16)
    ml_gate = jnp.pad(ml_w_in[:, :, ML_MAIN_WIDTH:],
                      ((0, 0), (0, 0), (0, LANES - 2 * ML_HEADS))).astype(BF16)
    ml_out = ml_w_out.astype(BF16)
    w_gu, w_down = ffn_w_gu.astype(BF16), ffn_w_down.astype(BF16)
    for i in range(DEPTH):
        kind, j = i % N_MIXERS, i // N_MIXERS
        conv = None
        if kind == 0:
            qkv = _norm_proj(x2d, ln_mix_pre[i], sb_qkv, j, BF16)
            mixed = _sb_attention(qkv.reshape(bsz, seq, 3 * d))
            w_out = sb_o
        elif kind == 1:
            mixed = _norm_proj(x2d, ln_mix_pre[i], gc_in, j, BF16)
            conv = (gc_conv_w[j], gc_conv_b[j], seq)
            w_out = gc_out
        else:
            main = _norm_proj(x2d, ln_mix_pre[i], ml_main, j, BF16)
            gates = _norm_proj(x2d, ln_mix_pre[i], ml_gate, j, F32)
            bias_row = jnp.pad(jnp.concatenate([ml_b_i[j], ml_b_f[j]]), (0, LANES - 2 * ML_HEADS))
            mixed = _mlstm_core(main.reshape(bsz, seq, ML_MAIN_WIDTH), gates.reshape(bsz, seq, LANES),
                                bias_row.reshape(1, LANES))
            w_out = ml_out
        x2d = _out_ffn(mixed.reshape(bsz * seq, -1), x2d, w_out, j, ln_mix_post[i], ln_ffn_pre[i],
                       w_gu, w_down, i, ln_ffn_post[i], conv)
    return x2d.reshape(bsz, seq, d)
```

```python
import functools

import jax
import jax.numpy as jnp
from jax import lax
from jax.experimental import pallas as pl
from jax.experimental.pallas import tpu as pltpu

D_MODEL = 1024
DEPTH = 4
N_MIXERS = 3
SB_HEADS = 16
SB_HEAD_DIM = 64
ML_HEADS = 8
ML_QK_DIM = 64
ML_V_DIM = 128
ML_QK_WIDTH = ML_HEADS * ML_QK_DIM
ML_MAIN_WIDTH = 2 * ML_QK_WIDTH + 2 * D_MODEL
ML_CHUNK = 64
D_FF = 2816
RMS_EPS = 1e-6

LANES = 128
ROW_BLOCK = 512
SB_BLOCK = 128
ML_STEP = 8 * ML_CHUNK
MXU_TILE = 256
FFN_CHUNKS = ((0, 6 * MXU_TILE), (6 * MXU_TILE, D_FF))
VMEM_LIMIT = 56 * 1024 * 1024

F32 = jnp.float32
BF16 = jnp.bfloat16


def _rms(x, g):
    ms = jnp.mean(x * x, axis=-1, keepdims=True)
    return x * lax.rsqrt(ms + RMS_EPS) * g


def _dot(a, b):
    return jnp.dot(a, b, preferred_element_type=F32)


def _dot_nt(a, b):
    return lax.dot_general(a, b, (((1,), (1,)), ((), ())), preferred_element_type=F32)


def _dot_tn(a, b):
    return lax.dot_general(a, b, (((0,), (0,)), ((), ())), preferred_element_type=F32)


def _split3(x):
    p0 = x.astype(BF16)
    r1 = x - p0.astype(F32)
    p1 = r1.astype(BF16)
    p2 = (r1 - p1.astype(F32)).astype(BF16)
    return p0, p1, p2


def _resident(shape):
    nd = len(shape)
    return pl.BlockSpec(shape, lambda *_: (0,) * nd, pipeline_mode=pl.Buffered(1))


def _resident_layer(stack, j):
    _, r, c = stack.shape
    return pl.BlockSpec((None, r, c), lambda *_: (j, 0, 0), pipeline_mode=pl.Buffered(1))


def _norm_proj_kernel(x_ref, g_ref, w_ref, *refs):
    casts = len(refs) // 2
    o_ref = refs[casts]
    hn = _rms(x_ref[...], g_ref[...]).astype(BF16)
    n = o_ref.shape[-1]
    step = D_MODEL if n % D_MODEL == 0 else n
    for c0 in range(0, n, step):
        o_ref[:, c0:c0 + step] = _dot(hn, w_ref[:, c0:c0 + step]).astype(o_ref.dtype)
    for src, dst in zip(refs[:casts], refs[casts + 1:]):
        dst[...] = src[...].astype(dst.dtype)


BF16_SUBLANES = 16


def _norm_proj(x2d, g, w_stack, j, out_dtype, cast=()):
    m, d = x2d.shape
    n = w_stack.shape[2]
    steps = m // ROW_BLOCK
    in_specs = [
        pl.BlockSpec((ROW_BLOCK, d), lambda i: (i, 0)),
        _resident((1, d)),
        _resident_layer(w_stack, j),
    ]
    out_specs = [pl.BlockSpec((ROW_BLOCK, n), lambda i: (i, 0))]
    out_shape = [jax.ShapeDtypeStruct((m, n), out_dtype)]
    for stack, layer in cast:
        _, r, c = stack.shape
        slabs = max(s for s in range(1, steps + 1) if r % s == 0 and (r // s) % BF16_SUBLANES == 0)
        in_specs.append(pl.BlockSpec(
            (None, r // slabs, c), lambda i, layer=layer, slabs=slabs: (layer, jnp.minimum(i, slabs - 1), 0)))
        out_specs.append(pl.BlockSpec((r // slabs, c), lambda i, slabs=slabs: (jnp.minimum(i, slabs - 1), 0)))
        out_shape.append(jax.ShapeDtypeStruct((r, c), BF16))
    outs = pl.pallas_call(
        _norm_proj_kernel,
        grid=(steps,),
        in_specs=in_specs,
        out_specs=out_specs,
        out_shape=out_shape,
        compiler_params=pltpu.CompilerParams(
            dimension_semantics=("arbitrary",), vmem_limit_bytes=VMEM_LIMIT),
        name="norm_proj",
    )(x2d, g.reshape(1, d), w_stack, *[stack for stack, _ in cast])
    return outs if cast else outs[0]


def _row_halves(rows):
    return [slice(k * rows // 2, (k + 1) * rows // 2) for k in range(2)]


def _ffn_tail(mixer_out, x_ref, wo_ref, gpost_ref, gpre_ref, wgu_ref, wd_ref, gfpost_ref, o_ref):
    halves = _row_halves(o_ref.shape[0])
    mixed = [_dot(mo, wo_ref[...]) for mo in mixer_out]
    x1 = [x_ref[h, :] + _rms(mx, gpost_ref[...]) for h, mx in zip(halves, mixed)]
    hn = [_rms(v, gpre_ref[...]).astype(BF16) for v in x1]
    acc = [None, None]
    for c0, c1 in FFN_CHUNKS:
        g = [_dot(v, wgu_ref[:, c0:c1]) for v in hn]
        u = [_dot(v, wgu_ref[:, D_FF + c0:D_FF + c1]) for v in hn]
        a = [(gk * (1.0 / (1.0 + jnp.exp(-gk))) * uk).astype(BF16) for gk, uk in zip(g, u)]
        part = [_dot(ak, wd_ref[c0:c1, :]) for ak in a]
        acc = [p if c is None else c + p for c, p in zip(acc, part)]
    for h, v, c in zip(halves, x1, acc):
        o_ref[h, :] = v + _rms(c, gfpost_ref[...])


def _out_ffn_kernel(m_ref, *refs):
    _ffn_tail([m_ref[h, :] for h in _row_halves(m_ref.shape[0])], *refs)


HALO = 16


def _conv_out_ffn_kernel(p_ref, h_ref, cw_ref, cb_ref, *refs, blocks_per_seq):
    d = D_MODEL
    rows = p_ref.shape[0]
    cu = p_ref[:, d:2 * d].astype(F32) * p_ref[:, 2 * d:].astype(F32)
    cu_prev = h_ref[:, d:2 * d].astype(F32) * h_ref[:, 2 * d:].astype(F32)
    cu_prev = jnp.where(pl.program_id(0) % blocks_per_seq == 0, 0.0, cu_prev)
    top_row = lax.broadcasted_iota(jnp.int32, (HALO, d), 0)

    def shifted(k):
        top = jnp.where(top_row < k, pltpu.roll(cu_prev, k, axis=0), pltpu.roll(cu[:HALO], k, axis=0))
        return jnp.concatenate([top, pltpu.roll(cu, k, axis=0)[HALO:]], axis=0)

    y = cw_ref[0:1, :] * shifted(2) + cw_ref[1:2, :] * shifted(1) + cw_ref[2:3, :] * cu + cb_ref[...]
    gated = (p_ref[:, :d].astype(F32) * y).astype(BF16)
    _ffn_tail([gated[h] for h in _row_halves(rows)], *refs)


def _out_ffn(mixer_in, x2d, wo, gpost, gpre, wgu, wd, gfpost, conv=None):
    m, d = x2d.shape
    row = lambda r: (r, 0)
    if conv is None:
        body, lead_specs, lead_args = _out_ffn_kernel, [pl.BlockSpec((ROW_BLOCK, d), row)], (mixer_in,)
    else:
        conv_w, conv_b, seq = conv
        body = functools.partial(_conv_out_ffn_kernel, blocks_per_seq=seq // ROW_BLOCK)
        lead_specs = [
            pl.BlockSpec((ROW_BLOCK, 3 * d), row),
            pl.BlockSpec((HALO, 3 * d), lambda r: (jnp.maximum(r * (ROW_BLOCK // HALO) - 1, 0), 0)),
            _resident((3, d)),
            _resident((1, d)),
        ]
        lead_args = (mixer_in, mixer_in, conv_w, conv_b.reshape(1, d))
    return pl.pallas_call(
        body,
        grid=(m // ROW_BLOCK,),
        in_specs=lead_specs + [
            pl.BlockSpec((ROW_BLOCK, d), row),
            _resident(wo.shape),
            _resident((1, d)),
            _resident((1, d)),
            _resident(wgu.shape),
            _resident(wd.shape),
            _resident((1, d)),
        ],
        out_specs=pl.BlockSpec((ROW_BLOCK, d), row),
        out_shape=jax.ShapeDtypeStruct((m, d), F32),
        compiler_params=pltpu.CompilerParams(
            dimension_semantics=("arbitrary",), vmem_limit_bytes=VMEM_LIMIT),
        name="out_ffn",
    )(*lead_args, x2d, wo, gpost.reshape(1, d), gpre.reshape(1, d), wgu, wd, gfpost.reshape(1, d))


SB_PAIRS = SB_HEADS // 2
SB_SKIP_MASS = 88.0


def _sb_kernel(q_ref, k_ref, v_ref, tri_ref, o_ref, q2_ref, carry_ref, acc_ref, least_ref):
    t = SB_BLOCK
    qi = pl.program_id(1)
    pairs = range(SB_PAIRS)
    cols = [slice(hp * LANES, (hp + 1) * LANES) for hp in pairs]
    lane_lo = lax.broadcasted_iota(jnp.int32, (t, LANES), 1) < SB_HEAD_DIM
    row = lax.broadcasted_iota(jnp.int32, (2 * t, t), 0)
    col = lax.broadcasted_iota(jnp.int32, (2 * t, t), 1)
    before = col < jnp.where(row >= t, row - t, row)

    for hp in pairs:
        qs = q_ref[0, :, cols[hp]] * jnp.asarray(SB_HEAD_DIM ** -0.5, BF16)
        q2_ref[hp] = jnp.concatenate([jnp.where(lane_lo, qs, 0), jnp.where(lane_lo, 0, qs)], axis=0)

    def sweep(tiles, fresh):
        starts = [pl.multiple_of(ti * t, t) for ti, _ in tiles]
        units = [(hp, k) for hp in pairs for k in range(len(tiles))]
        zs = {(hp, k): _dot_nt(q2_ref[hp], k_ref[0, pl.ds(starts[k], t), cols[hp]]) for hp, k in units}
        sps = {}
        for hp, k in units:
            z = zs[hp, k]
            sp = jnp.maximum(z, 0.0) + jnp.log(1.0 + jnp.exp(-jnp.abs(z)))
            if tiles[k][1] is not None:
                sp = jnp.where(tiles[k][1], sp, 0.0)
            sps[hp, k] = sp.astype(BF16)
        css = {u: _dot(sps[u], tri_ref[...]) for u in units}
        a_cats = []
        least = None
        for hp in pairs:
            carry = None if fresh else carry_ref[hp]
            parts = []
            for k, (_, mask) in enumerate(tiles):
                cs = css[hp, k]
                a = jnp.exp(zs[hp, k] - (cs[:, :t] if carry is None else cs[:, :t] + carry))
                if mask is not None:
                    a = jnp.where(mask, a, 0.0)
                a = a.astype(BF16)
                parts += [a[:t], a[t:]]
                carry = cs[:, t:] if carry is None else carry + cs[:, t:]
            a_cats.append(jnp.concatenate(parts, axis=1))
            carry_ref[hp] = carry
            least = carry if least is None else jnp.minimum(least, carry)
        for hp in pairs:
            vparts = []
            for k in range(len(tiles)):
                vt = v_ref[0, pl.ds(starts[k], t), cols[hp]]
                vparts += [jnp.where(lane_lo, vt, 0), jnp.where(lane_lo, 0, vt)]
            out = _dot(a_cats[hp], jnp.concatenate(vparts, axis=0))
            acc_ref[hp] = out if fresh else acc_ref[hp] + out
        return jnp.min(least)

    @pl.when(qi == 0)
    def _():
        least_ref[0] = sweep([(qi, before)], True)

    @pl.when(qi > 0)
    def _():
        least_ref[0] = sweep([(qi, before), (qi - 1, None)], True)

    lax.while_loop(lambda st: (st[0] <= qi) & (st[1] < SB_SKIP_MASS),
                   lambda st: (st[0] + 1, sweep([(qi - st[0], None)], False)),
                   (jnp.int32(2), least_ref[0]))
    for hp in pairs:
        o_ref[0, :, cols[hp]] = acc_ref[hp].astype(o_ref.dtype)


def _sb_tri():
    t = SB_BLOCK
    j = lax.broadcasted_iota(jnp.int32, (t, 2 * t), 0)
    s = lax.broadcasted_iota(jnp.int32, (t, 2 * t), 1)
    return ((j >= s) | (s >= t)).astype(BF16)


def _sb_attention(qkv):
    b, s, _ = qkv.shape
    t = SB_BLOCK
    return pl.pallas_call(
        _sb_kernel,
        grid=(b, s // t),
        in_specs=[
            pl.BlockSpec((1, t, D_MODEL), lambda bi, qi: (bi, qi, 0)),
            pl.BlockSpec((1, s, D_MODEL), lambda bi, qi: (bi, 0, 1)),
            pl.BlockSpec((1, s, D_MODEL), lambda bi, qi: (bi, 0, 2)),
            pl.BlockSpec((t, 2 * t), lambda bi, qi: (0, 0)),
        ],
        out_specs=pl.BlockSpec((1, t, D_MODEL), lambda bi, qi: (bi, qi, 0)),
        out_shape=jax.ShapeDtypeStruct((b, s, D_MODEL), BF16),
        scratch_shapes=[
            pltpu.VMEM((SB_PAIRS, 2 * t, LANES), BF16),
            pltpu.VMEM((SB_PAIRS, 2 * t, t), F32),
            pltpu.VMEM((SB_PAIRS, t, LANES), F32),
            pltpu.SMEM((1,), F32),
        ],
        compiler_params=pltpu.CompilerParams(
            dimension_semantics=("arbitrary", "arbitrary"), vmem_limit_bytes=VMEM_LIMIT),
        name="sb_attention",
    )(qkv, qkv, qkv, _sb_tri())


def _mlstm_expand():
    r = lax.broadcasted_iota(jnp.int32, (LANES, ML_HEADS * LANES), 0)
    c = lax.broadcasted_iota(jnp.int32, (LANES, ML_HEADS * LANES), 1)
    return (r == ML_HEADS + c // LANES).astype(BF16)


def _mlstm_kernel(p_ref, g_ref, bias_ref, e_ref, o_ref, st_ref, m_ref):
    L = ML_CHUNK
    heads = range(ML_HEADS)
    chunks = range(ML_STEP // L)

    @pl.when(pl.program_id(1) == 0)
    def _():
        st_ref[...] = jnp.zeros_like(st_ref)
        m_ref[...] = jnp.zeros_like(m_ref)

    r = lax.broadcasted_iota(jnp.int32, (L, L), 0)
    c = lax.broadcasted_iota(jnp.int32, (L, L), 1)
    causal = c <= r
    tril = causal.astype(BF16)
    eye = (lax.broadcasted_iota(jnp.int32, (LANES, LANES), 0)
           == lax.broadcasted_iota(jnp.int32, (LANES, LANES), 1)).astype(BF16)
    lane = lax.broadcasted_iota(jnp.int32, (L, LANES), 1)
    lane_lo = lane < ML_QK_DIM
    is_f = (lane >= ML_HEADS) & (lane < 2 * ML_HEADS)
    ones_col = (lane == 0).astype(BF16)

    m_all = m_ref[...]
    gate = []
    for ck in chunks:
        rows = slice(ck * L, (ck + 1) * L)
        g = g_ref[0, rows, :] + bias_ref[...]
        lf = jnp.where(is_f, jnp.minimum(g, 0.0) - jnp.log1p(jnp.exp(-jnp.abs(g))), 0.0)
        b_c = sum(_dot(tril, p) for p in _split3(lf))
        src = jnp.where(lane >= ML_HEADS, b_c, g)
        src_t = sum(_dot_nt(eye, jnp.concatenate([p, jnp.zeros_like(p)], axis=0))
                    for p in _split3(src))[:, :L]
        b_last = b_c[L - 1:L, :]
        log_w = b_last - b_c + pltpu.roll(g, ML_HEADS, axis=1)
        m_new = jnp.maximum(b_last + m_all, jnp.max(log_w, axis=0, keepdims=True))
        w_all = jnp.exp(log_w - m_new)
        decay = jnp.exp(b_last + m_all - m_new)
        stacked = jnp.concatenate([b_c, w_all, jnp.broadcast_to(decay, (4, LANES)),
                                   jnp.broadcast_to(m_all, (4, LANES))], axis=0)
        wide = sum(_dot(p, e_ref[...]) for p in _split3(stacked)[:2])
        gate.append((src_t, wide))
        m_all = m_new
    m_ref[...] = m_all

    units = [(ck, h) for ck in chunks for h in heads]
    qs, ks, vs, s_raw = {}, {}, {}, {}
    for ck, h in units:
        rows = slice(ck * L, (ck + 1) * L)
        hp, odd = h // 2, h % 2
        own = lane_lo != bool(odd)
        qs[ck, h] = jnp.where(own, p_ref[0, rows, hp * LANES:(hp + 1) * LANES], 0)
        kh = jnp.where(own, p_ref[0, rows, ML_QK_WIDTH + hp * LANES:ML_QK_WIDTH + (hp + 1) * LANES], 0)
        ks[ck, h] = kh * jnp.asarray(ML_QK_DIM ** -0.5, BF16)
        vh = p_ref[0, rows, 2 * ML_QK_WIDTH + h * LANES:2 * ML_QK_WIDTH + (h + 1) * LANES]
        vs[ck, h] = jnp.concatenate([vh, ones_col], axis=1)
        s_raw[ck, h] = _dot_nt(qs[ck, h], ks[ck, h])
    s_bf, inter, m_ts = {}, {}, {}
    for ck, h in units:
        src_t, wide = gate[ck]
        hb = slice(h * LANES, (h + 1) * LANES)
        b_col = wide[0:L, hb][:, :L]
        m_inter = b_col[:, :1] + wide[2 * L + 4:2 * L + 5, hb][:, :1]
        c_row = src_t[h:h + 1, :] - src_t[ML_HEADS + h:ML_HEADS + h + 1, :]
        log_d = jnp.where(causal, b_col + c_row, -jnp.inf)
        m_t = jnp.maximum(m_inter, jnp.max(log_d, axis=-1, keepdims=True))
        s_bf[ck, h] = (s_raw[ck, h] * jnp.exp(log_d - m_t)).astype(BF16)
        inter[ck, h] = jnp.exp(m_inter - m_t)
        m_ts[ck, h] = m_t
    sv = {u: _dot(s_bf[u], vs[u]) for u in units}

    for ck in chunks:
        rows = slice(ck * L, (ck + 1) * L)
        wide = gate[ck][1]
        qc = [_dot(qs[ck, h], st_ref[h].astype(BF16)) for h in heads]
        kws = []
        for h in heads:
            hb = slice(h * LANES, (h + 1) * LANES)
            tot = sv[ck, h] + inter[ck, h] * qc[h]
            den = tot[:, LANES:LANES + 1]
            h_out = tot[:, :LANES] / jnp.maximum(jnp.abs(den), jnp.exp(-m_ts[ck, h]))
            oh = p_ref[0, rows, 2 * ML_QK_WIDTH + D_MODEL + h * LANES:
                       2 * ML_QK_WIDTH + D_MODEL + (h + 1) * LANES].astype(F32)
            o_ref[0, rows, hb] = ((1.0 / (1.0 + jnp.exp(-oh))) * h_out).astype(o_ref.dtype)
            kws.append((ks[ck, h].astype(F32) * wide[L:2 * L, hb]).astype(BF16))
        upd = [_dot_tn(kws[h], vs[ck, h]) for h in heads]
        for h in heads:
            d_row = wide[2 * L:2 * L + 1, h * LANES:(h + 1) * LANES]
            st_ref[h] = jnp.concatenate([d_row, d_row], axis=1) * st_ref[h] + upd[h]


def _mlstm_core(main, gates, bias_row):
    b, s, n = main.shape
    return pl.pallas_call(
        _mlstm_kernel,
        grid=(b, s // ML_STEP),
        in_specs=[
            pl.BlockSpec((1, ML_STEP, n), lambda bi, i: (bi, i, 0)),
            pl.BlockSpec((1, ML_STEP, LANES), lambda bi, i: (bi, i, 0)),
            pl.BlockSpec((1, LANES), lambda bi, i: (0, 0)),
            pl.BlockSpec((LANES, ML_HEADS * LANES), lambda bi, i: (0, 0)),
        ],
        out_specs=pl.BlockSpec((1, ML_STEP, D_MODEL), lambda bi, i: (bi, i, 0)),
        out_shape=jax.ShapeDtypeStruct((b, s, D_MODEL), BF16),
        scratch_shapes=[
            pltpu.VMEM((ML_HEADS, LANES, 2 * LANES), F32),
            pltpu.VMEM((1, LANES), F32),
        ],
        compiler_params=pltpu.CompilerParams(dimension_semantics=("arbitrary", "arbitrary")),
        name="mlstm_core",
    )(main, gates, bias_row, _mlstm_expand())


def kernel(x, ln_mix_pre, ln_mix_post, ln_ffn_pre, ln_ffn_post, sb_w_qkv, sb_w_o, gc_w_in, gc_conv_w,
           gc_conv_b, gc_w_out, ml_w_in, ml_b_i, ml_b_f, ml_w_out, ffn_w_gu, ffn_w_down):
    bsz, seq, d = x.shape
    x2d = x.reshape(bsz * seq, d)
    sb_qkv, gc_in = sb_w_qkv.astype(BF16), gc_w_in.astype(BF16)
    ml_main = ml_w_in[:, :, :ML_MAIN_WIDTH].astype(BF16)
    ml_gate = jnp.pad(ml_w_in[:, :, ML_MAIN_WIDTH:],
                      ((0, 0), (0, 0), (0, LANES - 2 * ML_HEADS))).astype(BF16)
    for i in range(DEPTH):
        kind, j = i % N_MIXERS, i // N_MIXERS
        w_in, w_out = ((sb_qkv, sb_w_o), (gc_in, gc_w_out), (ml_main, ml_w_out))[kind]
        proj, wo, wgu, wd = _norm_proj(x2d, ln_mix_pre[i], w_in, j, BF16,
                                       cast=((w_out, j), (ffn_w_gu, i), (ffn_w_down, i)))
        conv = None
        if kind == 0:
            mixed = _sb_attention(proj.reshape(bsz, seq, 3 * d))
        elif kind == 1:
            mixed = proj
            conv = (gc_conv_w[j], gc_conv_b[j], seq)
        else:
            gates = _norm_proj(x2d, ln_mix_pre[i], ml_gate, j, F32)
            bias_row = jnp.pad(jnp.concatenate([ml_b_i[j], ml_b_f[j]]), (0, LANES - 2 * ML_HEADS))
            mixed = _mlstm_core(proj.reshape(bsz, seq, ML_MAIN_WIDTH), gates.reshape(bsz, seq, LANES),
                                bias_row.reshape(1, LANES))
        x2d = _out_ffn(mixed.reshape(bsz * seq, -1), x2d, wo, ln_mix_post[i], ln_ffn_pre[i],
                       wgu, wd, ln_ffn_post[i], conv)
    return x2d.reshape(bsz, seq, d)
```

```python
import functools

import jax
import jax.numpy as jnp
from jax import lax
from jax.experimental import pallas as pl
from jax.experimental.pallas import tpu as pltpu

D_MODEL = 1024
DEPTH = 4
N_MIXERS = 3
SB_HEADS = 16
SB_HEAD_DIM = 64
ML_HEADS = 8
ML_QK_DIM = 64
ML_V_DIM = 128
ML_QK_WIDTH = ML_HEADS * ML_QK_DIM
ML_MAIN_WIDTH = 2 * ML_QK_WIDTH + 2 * D_MODEL
ML_CHUNK = 64
D_FF = 2816
RMS_EPS = 1e-6

LANES = 128
ROW_BLOCK = 512
SB_BLOCK = 128
ML_STEP = 8 * ML_CHUNK
MXU_TILE = 256
FFN_CHUNKS = ((0, 6 * MXU_TILE), (6 * MXU_TILE, D_FF))
VMEM_LIMIT = 56 * 1024 * 1024

F32 = jnp.float32
BF16 = jnp.bfloat16


def _rms(x, g):
    ms = jnp.mean(x * x, axis=-1, keepdims=True)
    return x * lax.rsqrt(ms + RMS_EPS) * g


def _dot(a, b):
    return jnp.dot(a, b, preferred_element_type=F32)


def _dot_nt(a, b):
    return lax.dot_general(a, b, (((1,), (1,)), ((), ())), preferred_element_type=F32)


def _dot_tn(a, b):
    return lax.dot_general(a, b, (((0,), (0,)), ((), ())), preferred_element_type=F32)


def _split3(x):
    p0 = x.astype(BF16)
    r1 = x - p0.astype(F32)
    p1 = r1.astype(BF16)
    p2 = (r1 - p1.astype(F32)).astype(BF16)
    return p0, p1, p2


def _resident(shape):
    nd = len(shape)
    return pl.BlockSpec(shape, lambda *_: (0,) * nd, pipeline_mode=pl.Buffered(1))


def _resident_layer(stack, j):
    _, r, c = stack.shape
    return pl.BlockSpec((None, r, c), lambda *_: (j, 0, 0), pipeline_mode=pl.Buffered(1))


def _norm_proj_kernel(x_ref, g_ref, *refs, n_proj):
    n_cast = len(refs) // 2 - n_proj
    ws, srcs = refs[:n_proj], refs[n_proj:n_proj + n_cast]
    outs, dsts = refs[n_proj + n_cast:2 * n_proj + n_cast], refs[2 * n_proj + n_cast:]
    hn = _rms(x_ref[...], g_ref[...]).astype(BF16)
    for w_ref, o_ref in zip(ws, outs):
        n = o_ref.shape[-1]
        step = D_MODEL if n % D_MODEL == 0 else n
        for c0 in range(0, n, step):
            o_ref[:, c0:c0 + step] = _dot(hn, w_ref[:, c0:c0 + step]).astype(o_ref.dtype)
    for src, dst in zip(srcs, dsts):
        dst[...] = src[...].astype(dst.dtype)


BF16_SUBLANES = 16
NP_ROW_BLOCK = 2 * ROW_BLOCK


def _norm_proj(x2d, g, projs, cast=()):
    m, d = x2d.shape
    steps = m // NP_ROW_BLOCK
    row = lambda i: (i, 0)
    in_specs = [pl.BlockSpec((NP_ROW_BLOCK, d), row), _resident((1, d))]
    in_specs += [_resident_layer(w, j) for w, j, _ in projs]
    out_specs = [pl.BlockSpec((NP_ROW_BLOCK, w.shape[2]), row) for w, _, _ in projs]
    out_shape = [jax.ShapeDtypeStruct((m, w.shape[2]), dt) for w, _, dt in projs]
    for stack, layer in cast:
        _, r, c = stack.shape
        slabs = max(s for s in range(1, steps + 1) if r % s == 0 and (r // s) % BF16_SUBLANES == 0)
        in_specs.append(pl.BlockSpec(
            (None, r // slabs, c), lambda i, layer=layer, slabs=slabs: (layer, jnp.minimum(i, slabs - 1), 0)))
        out_specs.append(pl.BlockSpec((r // slabs, c), lambda i, slabs=slabs: (jnp.minimum(i, slabs - 1), 0)))
        out_shape.append(jax.ShapeDtypeStruct((r, c), BF16))
    return pl.pallas_call(
        functools.partial(_norm_proj_kernel, n_proj=len(projs)),
        grid=(steps,),
        in_specs=in_specs,
        out_specs=out_specs,
        out_shape=out_shape,
        compiler_params=pltpu.CompilerParams(
            dimension_semantics=("arbitrary",), vmem_limit_bytes=VMEM_LIMIT),
        name="norm_proj",
    )(x2d, g.reshape(1, d), *[w for w, _, _ in projs], *[stack for stack, _ in cast])


def _row_halves(rows):
    return [slice(k * rows // 2, (k + 1) * rows // 2) for k in range(2)]


def _ffn_tail(mixer_out, x_ref, wo_ref, gpost_ref, gpre_ref, wgu_ref, wd_ref, gfpost_ref, o_ref):
    halves = _row_halves(o_ref.shape[0])
    mixed = [_dot(mo, wo_ref[...]) for mo in mixer_out]
    x1 = [x_ref[h, :] + _rms(mx, gpost_ref[...]) for h, mx in zip(halves, mixed)]
    hn = [_rms(v, gpre_ref[...]).astype(BF16) for v in x1]
    acc = [None, None]
    for c0, c1 in FFN_CHUNKS:
        g = [_dot(v, wgu_ref[:, c0:c1]) for v in hn]
        u = [_dot(v, wgu_ref[:, D_FF + c0:D_FF + c1]) for v in hn]
        a = [(gk * (1.0 / (1.0 + jnp.exp(-gk))) * uk).astype(BF16) for gk, uk in zip(g, u)]
        part = [_dot(ak, wd_ref[c0:c1, :]) for ak in a]
        acc = [p if c is None else c + p for c, p in zip(acc, part)]
    for h, v, c in zip(halves, x1, acc):
        o_ref[h, :] = v + _rms(c, gfpost_ref[...])


def _out_ffn_kernel(m_ref, *refs):
    _ffn_tail([m_ref[h, :] for h in _row_halves(m_ref.shape[0])], *refs)


HALO = 16


def _conv_out_ffn_kernel(p_ref, h_ref, cw_ref, cb_ref, *refs, blocks_per_seq):
    d = D_MODEL
    rows = p_ref.shape[0]
    cu = p_ref[:, d:2 * d].astype(F32) * p_ref[:, 2 * d:].astype(F32)
    cu_prev = h_ref[:, d:2 * d].astype(F32) * h_ref[:, 2 * d:].astype(F32)
    cu_prev = jnp.where(pl.program_id(0) % blocks_per_seq == 0, 0.0, cu_prev)
    top_row = lax.broadcasted_iota(jnp.int32, (HALO, d), 0)

    def shifted(k):
        top = jnp.where(top_row < k, pltpu.roll(cu_prev, k, axis=0), pltpu.roll(cu[:HALO], k, axis=0))
        return jnp.concatenate([top, pltpu.roll(cu, k, axis=0)[HALO:]], axis=0)

    y = cw_ref[0:1, :] * shifted(2) + cw_ref[1:2, :] * shifted(1) + cw_ref[2:3, :] * cu + cb_ref[...]
    gated = (p_ref[:, :d].astype(F32) * y).astype(BF16)
    _ffn_tail([gated[h] for h in _row_halves(rows)], *refs)


def _out_ffn(mixer_in, x2d, wo, gpost, gpre, wgu, wd, gfpost, conv=None):
    m, d = x2d.shape
    row = lambda r: (r, 0)
    if conv is None:
        body, lead_specs, lead_args = _out_ffn_kernel, [pl.BlockSpec((ROW_BLOCK, d), row)], (mixer_in,)
    else:
        conv_w, conv_b, seq = conv
        body = functools.partial(_conv_out_ffn_kernel, blocks_per_seq=seq // ROW_BLOCK)
        lead_specs = [
            pl.BlockSpec((ROW_BLOCK, 3 * d), row),
            pl.BlockSpec((HALO, 3 * d), lambda r: (jnp.maximum(r * (ROW_BLOCK // HALO) - 1, 0), 0)),
            _resident((3, d)),
            _resident((1, d)),
        ]
        lead_args = (mixer_in, mixer_in, conv_w, conv_b.reshape(1, d))
    return pl.pallas_call(
        body,
        grid=(m // ROW_BLOCK,),
        in_specs=lead_specs + [
            pl.BlockSpec((ROW_BLOCK, d), row),
            _resident(wo.shape),
            _resident((1, d)),
            _resident((1, d)),
            _resident(wgu.shape),
            _resident(wd.shape),
            _resident((1, d)),
        ],
        out_specs=pl.BlockSpec((ROW_BLOCK, d), row),
        out_shape=jax.ShapeDtypeStruct((m, d), F32),
        compiler_params=pltpu.CompilerParams(
            dimension_semantics=("arbitrary",), vmem_limit_bytes=VMEM_LIMIT),
        name="out_ffn",
    )(*lead_args, x2d, wo, gpost.reshape(1, d), gpre.reshape(1, d), wgu, wd, gfpost.reshape(1, d))


SB_PAIRS = SB_HEADS // 2
SB_SKIP_MASS = 88.0


def _sb_kernel(q_ref, k_ref, v_ref, tri_ref, o_ref, q2_ref, carry_ref, acc_ref, least_ref):
    t = SB_BLOCK
    qi = pl.program_id(1)
    pairs = range(SB_PAIRS)
    cols = [slice(hp * LANES, (hp + 1) * LANES) for hp in pairs]
    lane_lo = lax.broadcasted_iota(jnp.int32, (t, LANES), 1) < SB_HEAD_DIM
    row = lax.broadcasted_iota(jnp.int32, (2 * t, t), 0)
    col = lax.broadcasted_iota(jnp.int32, (2 * t, t), 1)
    before = col < jnp.where(row >= t, row - t, row)

    for hp in pairs:
        qs = q_ref[0, :, cols[hp]] * jnp.asarray(SB_HEAD_DIM ** -0.5, BF16)
        q2_ref[hp] = jnp.concatenate([jnp.where(lane_lo, qs, 0), jnp.where(lane_lo, 0, qs)], axis=0)

    def sweep(tiles, fresh):
        starts = [pl.multiple_of(ti * t, t) for ti, _ in tiles]
        units = [(hp, k) for hp in pairs for k in range(len(tiles))]
        zs = {(hp, k): _dot_nt(q2_ref[hp], k_ref[0, pl.ds(starts[k], t), cols[hp]]) for hp, k in units}
        sps = {}
        for hp, k in units:
            z = zs[hp, k]
            sp = jnp.maximum(z, 0.0) + jnp.log(1.0 + jnp.exp(-jnp.abs(z)))
            if tiles[k][1] is not None:
                sp = jnp.where(tiles[k][1], sp, 0.0)
            sps[hp, k] = sp.astype(BF16)
        css = {u: _dot(sps[u], tri_ref[...]) for u in units}
        a_cats = []
        least = None
        for hp in pairs:
            carry = None if fresh else carry_ref[hp]
            parts = []
            for k, (_, mask) in enumerate(tiles):
                cs = css[hp, k]
                a = jnp.exp(zs[hp, k] - (cs[:, :t] if carry is None else cs[:, :t] + carry))
                if mask is not None:
                    a = jnp.where(mask, a, 0.0)
                a = a.astype(BF16)
                parts += [a[:t], a[t:]]
                carry = cs[:, t:] if carry is None else carry + cs[:, t:]
            a_cats.append(jnp.concatenate(parts, axis=1))
            carry_ref[hp] = carry
            least = carry if least is None else jnp.minimum(least, carry)
        for hp in pairs:
            vparts = []
            for k in range(len(tiles)):
                vt = v_ref[0, pl.ds(starts[k], t), cols[hp]]
                vparts += [jnp.where(lane_lo, vt, 0), jnp.where(lane_lo, 0, vt)]
            out = _dot(a_cats[hp], jnp.concatenate(vparts, axis=0))
            acc_ref[hp] = out if fresh else acc_ref[hp] + out
        return jnp.min(least)

    @pl.when(qi == 0)
    def _():
        least_ref[0] = sweep([(qi, before)], True)

    @pl.when(qi > 0)
    def _():
        least_ref[0] = sweep([(qi, before), (qi - 1, None)], True)

    lax.while_loop(lambda st: (st[0] <= qi) & (st[1] < SB_SKIP_MASS),
                   lambda st: (st[0] + 1, sweep([(qi - st[0], None)], False)),
                   (jnp.int32(2), least_ref[0]))
    for hp in pairs:
        o_ref[0, :, cols[hp]] = acc_ref[hp].astype(o_ref.dtype)


def _sb_tri():
    t = SB_BLOCK
    j = lax.broadcasted_iota(jnp.int32, (t, 2 * t), 0)
    s = lax.broadcasted_iota(jnp.int32, (t, 2 * t), 1)
    return ((j >= s) | (s >= t)).astype(BF16)


def _sb_attention(qkv):
    b, s, _ = qkv.shape
    t = SB_BLOCK
    return pl.pallas_call(
        _sb_kernel,
        grid=(b, s // t),
        in_specs=[
            pl.BlockSpec((1, t, D_MODEL), lambda bi, qi: (bi, qi, 0)),
            pl.BlockSpec((1, s, D_MODEL), lambda bi, qi: (bi, 0, 1)),
            pl.BlockSpec((1, s, D_MODEL), lambda bi, qi: (bi, 0, 2)),
            pl.BlockSpec((t, 2 * t), lambda bi, qi: (0, 0)),
        ],
        out_specs=pl.BlockSpec((1, t, D_MODEL), lambda bi, qi: (bi, qi, 0)),
        out_shape=jax.ShapeDtypeStruct((b, s, D_MODEL), BF16),
        scratch_shapes=[
            pltpu.VMEM((SB_PAIRS, 2 * t, LANES), BF16),
            pltpu.VMEM((SB_PAIRS, 2 * t, t), F32),
            pltpu.VMEM((SB_PAIRS, t, LANES), F32),
            pltpu.SMEM((1,), F32),
        ],
        compiler_params=pltpu.CompilerParams(
            dimension_semantics=("arbitrary", "arbitrary"), vmem_limit_bytes=VMEM_LIMIT),
        name="sb_attention",
    )(qkv, qkv, qkv, _sb_tri())


def _mlstm_expand():
    r = lax.broadcasted_iota(jnp.int32, (LANES, ML_HEADS * LANES), 0)
    c = lax.broadcasted_iota(jnp.int32, (LANES, ML_HEADS * LANES), 1)
    return (r == ML_HEADS + c // LANES).astype(BF16)


def _mlstm_kernel(p_ref, g_ref, bias_ref, e_ref, o_ref, st_ref, m_ref):
    L = ML_CHUNK
    heads = range(ML_HEADS)
    chunks = range(ML_STEP // L)

    @pl.when(pl.program_id(1) == 0)
    def _():
        st_ref[...] = jnp.zeros_like(st_ref)
        m_ref[...] = jnp.zeros_like(m_ref)

    r = lax.broadcasted_iota(jnp.int32, (L, L), 0)
    c = lax.broadcasted_iota(jnp.int32, (L, L), 1)
    causal = c <= r
    tril = causal.astype(BF16)
    eye = (lax.broadcasted_iota(jnp.int32, (LANES, LANES), 0)
           == lax.broadcasted_iota(jnp.int32, (LANES, LANES), 1)).astype(BF16)
    lane = lax.broadcasted_iota(jnp.int32, (L, LANES), 1)
    lane_lo = lane < ML_QK_DIM
    is_f = (lane >= ML_HEADS) & (lane < 2 * ML_HEADS)
    ones_col = (lane == 0).astype(BF16)

    m_all = m_ref[...]
    gate = []
    for ck in chunks:
        rows = slice(ck * L, (ck + 1) * L)
        g = g_ref[0, rows, :] + bias_ref[...]
        lf = jnp.where(is_f, jnp.minimum(g, 0.0) - jnp.log1p(jnp.exp(-jnp.abs(g))), 0.0)
        b_c = sum(_dot(tril, p) for p in _split3(lf))
        src = jnp.where(lane >= ML_HEADS, b_c, g)
        src_t = sum(_dot_nt(eye, jnp.concatenate([p, jnp.zeros_like(p)], axis=0))
                    for p in _split3(src))[:, :L]
        b_last = b_c[L - 1:L, :]
        log_w = b_last - b_c + pltpu.roll(g, ML_HEADS, axis=1)
        m_new = jnp.maximum(b_last + m_all, jnp.max(log_w, axis=0, keepdims=True))
        w_all = jnp.exp(log_w - m_new)
        decay = jnp.exp(b_last + m_all - m_new)
        stacked = jnp.concatenate([b_c, w_all, jnp.broadcast_to(decay, (4, LANES)),
                                   jnp.broadcast_to(m_all, (4, LANES))], axis=0)
        wide = sum(_dot(p, e_ref[...]) for p in _split3(stacked)[:2])
        gate.append((src_t, wide))
        m_all = m_new
    m_ref[...] = m_all

    units = [(ck, h) for ck in chunks for h in heads]
    qs, ks, vs, s_raw = {}, {}, {}, {}
    for ck, h in units:
        rows = slice(ck * L, (ck + 1) * L)
        hp, odd = h // 2, h % 2
        own = lane_lo != bool(odd)
        qs[ck, h] = jnp.where(own, p_ref[0, rows, hp * LANES:(hp + 1) * LANES], 0)
        kh = jnp.where(own, p_ref[0, rows, ML_QK_WIDTH + hp * LANES:ML_QK_WIDTH + (hp + 1) * LANES], 0)
        ks[ck, h] = kh * jnp.asarray(ML_QK_DIM ** -0.5, BF16)
        vh = p_ref[0, rows, 2 * ML_QK_WIDTH + h * LANES:2 * ML_QK_WIDTH + (h + 1) * LANES]
        vs[ck, h] = jnp.concatenate([vh, ones_col], axis=1)
        s_raw[ck, h] = _dot_nt(qs[ck, h], ks[ck, h])
    s_bf, inter, m_ts = {}, {}, {}
    for ck, h in units:
        src_t, wide = gate[ck]
        hb = slice(h * LANES, (h + 1) * LANES)
        b_col = wide[0:L, hb][:, :L]
        m_inter = b_col[:, :1] + wide[2 * L + 4:2 * L + 5, hb][:, :1]
        c_row = src_t[h:h + 1, :] - src_t[ML_HEADS + h:ML_HEADS + h + 1, :]
        log_d = jnp.where(causal, b_col + c_row, -jnp.inf)
        m_t = jnp.maximum(m_inter, jnp.max(log_d, axis=-1, keepdims=True))
        s_bf[ck, h] = (s_raw[ck, h] * jnp.exp(log_d - m_t)).astype(BF16)
        inter[ck, h] = jnp.exp(m_inter - m_t)
        m_ts[ck, h] = m_t
    sv = {u: _dot(s_bf[u], vs[u]) for u in units}

    for ck in chunks:
        rows = slice(ck * L, (ck + 1) * L)
        wide = gate[ck][1]
        qc = [_dot(qs[ck, h], st_ref[h].astype(BF16)) for h in heads]
        kws = []
        for h in heads:
            hb = slice(h * LANES, (h + 1) * LANES)
            tot = sv[ck, h] + inter[ck, h] * qc[h]
            den = tot[:, LANES:LANES + 1]
            h_out = tot[:, :LANES] / jnp.maximum(jnp.abs(den), jnp.exp(-m_ts[ck, h]))
            oh = p_ref[0, rows, 2 * ML_QK_WIDTH + D_MODEL + h * LANES:
                       2 * ML_QK_WIDTH + D_MODEL + (h + 1) * LANES].astype(F32)
            o_ref[0, rows, hb] = ((1.0 / (1.0 + jnp.exp(-oh))) * h_out).astype(o_ref.dtype)
            kws.append((ks[ck, h].astype(F32) * wide[L:2 * L, hb]).astype(BF16))
        upd = [_dot_tn(kws[h], vs[ck, h]) for h in heads]
        for h in heads:
            d_row = wide[2 * L:2 * L + 1, h * LANES:(h + 1) * LANES]
            st_ref[h] = jnp.concatenate([d_row, d_row], axis=1) * st_ref[h] + upd[h]


def _mlstm_core(main, gates, bias_row):
    b, s, n = main.shape
    return pl.pallas_call(
        _mlstm_kernel,
        grid=(b, s // ML_STEP),
        in_specs=[
            pl.BlockSpec((1, ML_STEP, n), lambda bi, i: (bi, i, 0)),
            pl.BlockSpec((1, ML_STEP, LANES), lambda bi, i: (bi, i, 0)),
            pl.BlockSpec((1, LANES), lambda bi, i: (0, 0)),
            pl.BlockSpec((LANES, ML_HEADS * LANES), lambda bi, i: (0, 0)),
        ],
        out_specs=pl.BlockSpec((1, ML_STEP, D_MODEL), lambda bi, i: (bi, i, 0)),
        out_shape=jax.ShapeDtypeStruct((b, s, D_MODEL), BF16),
        scratch_shapes=[
            pltpu.VMEM((ML_HEADS, LANES, 2 * LANES), F32),
            pltpu.VMEM((1, LANES), F32),
        ],
        compiler_params=pltpu.CompilerParams(dimension_semantics=("arbitrary", "arbitrary")),
        name="mlstm_core",
    )(main, gates, bias_row, _mlstm_expand())


def kernel(x, ln_mix_pre, ln_mix_post, ln_ffn_pre, ln_ffn_post, sb_w_qkv, sb_w_o, gc_w_in, gc_conv_w,
           gc_conv_b, gc_w_out, ml_w_in, ml_b_i, ml_b_f, ml_w_out, ffn_w_gu, ffn_w_down):
    bsz, seq, d = x.shape
    x2d = x.reshape(bsz * seq, d)
    sb_qkv, gc_in = sb_w_qkv.astype(BF16), gc_w_in.astype(BF16)
    ml_main = ml_w_in[:, :, :ML_MAIN_WIDTH].astype(BF16)
    ml_gate = jnp.pad(ml_w_in[:, :, ML_MAIN_WIDTH:],
                      ((0, 0), (0, 0), (0, LANES - 2 * ML_HEADS))).astype(BF16)
    for i in range(DEPTH):
        kind, j = i % N_MIXERS, i // N_MIXERS
        w_in, w_out = ((sb_qkv, sb_w_o), (gc_in, gc_w_out), (ml_main, ml_w_out))[kind]
        projs = [(w_in, j, BF16)] + ([(ml_gate, j, F32)] if kind == 2 else [])
        proj, *rest = _norm_proj(x2d, ln_mix_pre[i], projs,
                                 cast=((w_out, j), (ffn_w_gu, i), (ffn_w_down, i)))
        wo, wgu, wd = rest[-3:]
        conv = None
        if kind == 0:
            mixed = _sb_attention(proj.reshape(bsz, seq, 3 * d))
        elif kind == 1:
            mixed = proj
            conv = (gc_conv_w[j], gc_conv_b[j], seq)
        else:
            bias_row = jnp.pad(jnp.concatenate([ml_b_i[j], ml_b_f[j]]), (0, LANES - 2 * ML_HEADS))
            mixed = _mlstm_core(proj.reshape(bsz, seq, ML_MAIN_WIDTH), rest[0].reshape(bsz, seq, LANES),
                                bias_row.reshape(1, LANES))
        x2d = _out_ffn(mixed.reshape(bsz * seq, -1), x2d, wo, ln_mix_post[i], ln_ffn_pre[i],
                       wgu, wd, ln_ffn_post[i], conv)
    return x2d.reshape(bsz, seq, d)
```

```python
import functools

import jax
import jax.numpy as jnp
from jax import lax
from jax.experimental import pallas as pl
from jax.experimental.pallas import tpu as pltpu

D_MODEL = 1024
DEPTH = 4
N_MIXERS = 3
SB_HEADS = 16
SB_HEAD_DIM = 64
ML_HEADS = 8
ML_QK_DIM = 64
ML_V_DIM = 128
ML_QK_WIDTH = ML_HEADS * ML_QK_DIM
ML_MAIN_WIDTH = 2 * ML_QK_WIDTH + 2 * D_MODEL
ML_CHUNK = 64
D_FF = 2816
RMS_EPS = 1e-6

LANES = 128
ROW_BLOCK = 512
SB_BLOCK = 128
ML_STEP = 8 * ML_CHUNK
MXU_TILE = 256
FFN_CHUNKS = ((0, 6 * MXU_TILE), (6 * MXU_TILE, D_FF))
VMEM_LIMIT = 56 * 1024 * 1024

F32 = jnp.float32
BF16 = jnp.bfloat16


def _rms(x, g):
    ms = jnp.mean(x * x, axis=-1, keepdims=True)
    return x * lax.rsqrt(ms + RMS_EPS) * g


def _dot(a, b):
    return jnp.dot(a, b, preferred_element_type=F32)


def _dot_nt(a, b):
    return lax.dot_general(a, b, (((1,), (1,)), ((), ())), preferred_element_type=F32)


def _dot_tn(a, b):
    return lax.dot_general(a, b, (((0,), (0,)), ((), ())), preferred_element_type=F32)


def _split3(x):
    p0 = x.astype(BF16)
    r1 = x - p0.astype(F32)
    p1 = r1.astype(BF16)
    p2 = (r1 - p1.astype(F32)).astype(BF16)
    return p0, p1, p2


def _resident(shape):
    nd = len(shape)
    return pl.BlockSpec(shape, lambda *_: (0,) * nd, pipeline_mode=pl.Buffered(1))


def _resident_layer(stack, j):
    _, r, c = stack.shape
    return pl.BlockSpec((None, r, c), lambda *_: (j, 0, 0), pipeline_mode=pl.Buffered(1))


def _norm_proj_kernel(x_ref, g_ref, *refs, n_proj):
    n_cast = len(refs) // 2 - n_proj
    ws, srcs = refs[:n_proj], refs[n_proj:n_proj + n_cast]
    outs, dsts = refs[n_proj + n_cast:2 * n_proj + n_cast], refs[2 * n_proj + n_cast:]
    hn = _rms(x_ref[...], g_ref[...]).astype(BF16)
    for w_ref, o_ref in zip(ws, outs):
        n = o_ref.shape[-1]
        step = D_MODEL if n % D_MODEL == 0 else n
        for c0 in range(0, n, step):
            o_ref[:, c0:c0 + step] = _dot(hn, w_ref[:, c0:c0 + step]).astype(o_ref.dtype)
    for src, dst in zip(srcs, dsts):
        dst[...] = src[...].astype(dst.dtype)


BF16_SUBLANES = 16
NP_ROW_BLOCK = 2 * ROW_BLOCK


def _norm_proj(x2d, g, projs, cast=()):
    m, d = x2d.shape
    steps = m // NP_ROW_BLOCK
    row = lambda i: (i, 0)
    in_specs = [pl.BlockSpec((NP_ROW_BLOCK, d), row), _resident((1, d))]
    in_specs += [_resident_layer(w, j) for w, j, _ in projs]
    out_specs = [pl.BlockSpec((NP_ROW_BLOCK, w.shape[2]), row) for w, _, _ in projs]
    out_shape = [jax.ShapeDtypeStruct((m, w.shape[2]), dt) for w, _, dt in projs]
    for stack, layer, *cols in cast:
        r, c = stack.shape[1], (cols[0] if cols else stack.shape[2])
        slabs = max(s for s in range(1, steps + 1) if r % s == 0 and (r // s) % BF16_SUBLANES == 0)
        in_specs.append(pl.BlockSpec(
            (None, r // slabs, c), lambda i, layer=layer, slabs=slabs: (layer, jnp.minimum(i, slabs - 1), 0)))
        out_specs.append(pl.BlockSpec((r // slabs, c), lambda i, slabs=slabs: (jnp.minimum(i, slabs - 1), 0)))
        out_shape.append(jax.ShapeDtypeStruct((r, c), BF16))
    return pl.pallas_call(
        functools.partial(_norm_proj_kernel, n_proj=len(projs)),
        grid=(steps,),
        in_specs=in_specs,
        out_specs=out_specs,
        out_shape=out_shape,
        compiler_params=pltpu.CompilerParams(
            dimension_semantics=("arbitrary",), vmem_limit_bytes=VMEM_LIMIT),
        name="norm_proj",
    )(x2d, g.reshape(1, d), *[w for w, _, _ in projs], *[entry[0] for entry in cast])


def _row_halves(rows):
    return [slice(k * rows // 2, (k + 1) * rows // 2) for k in range(2)]


def _ffn_tail(mixer_out, x_ref, wo_ref, gpost_ref, gpre_ref, wgu_ref, wd_ref, gfpost_ref, o_ref):
    halves = _row_halves(o_ref.shape[0])
    mixed = [_dot(mo, wo_ref[...]) for mo in mixer_out]
    x1 = [x_ref[h, :] + _rms(mx, gpost_ref[...]) for h, mx in zip(halves, mixed)]
    hn = [_rms(v, gpre_ref[...]).astype(BF16) for v in x1]
    acc = [None, None]
    for c0, c1 in FFN_CHUNKS:
        g = [_dot(v, wgu_ref[:, c0:c1]) for v in hn]
        u = [_dot(v, wgu_ref[:, D_FF + c0:D_FF + c1]) for v in hn]
        a = [(gk * (1.0 / (1.0 + jnp.exp(-gk))) * uk).astype(BF16) for gk, uk in zip(g, u)]
        part = [_dot(ak, wd_ref[c0:c1, :]) for ak in a]
        acc = [p if c is None else c + p for c, p in zip(acc, part)]
    for h, v, c in zip(halves, x1, acc):
        o_ref[h, :] = v + _rms(c, gfpost_ref[...])


def _out_ffn_kernel(m_ref, *refs):
    _ffn_tail([m_ref[h, :] for h in _row_halves(m_ref.shape[0])], *refs)


HALO = 16


def _conv_out_ffn_kernel(p_ref, h_ref, cw_ref, cb_ref, *refs, blocks_per_seq):
    d = D_MODEL
    rows = p_ref.shape[0]
    cu = p_ref[:, d:2 * d].astype(F32) * p_ref[:, 2 * d:].astype(F32)
    cu_prev = h_ref[:, d:2 * d].astype(F32) * h_ref[:, 2 * d:].astype(F32)
    cu_prev = jnp.where(pl.program_id(0) % blocks_per_seq == 0, 0.0, cu_prev)
    top_row = lax.broadcasted_iota(jnp.int32, (HALO, d), 0)

    def shifted(k):
        top = jnp.where(top_row < k, pltpu.roll(cu_prev, k, axis=0), pltpu.roll(cu[:HALO], k, axis=0))
        return jnp.concatenate([top, pltpu.roll(cu, k, axis=0)[HALO:]], axis=0)

    y = cw_ref[0:1, :] * shifted(2) + cw_ref[1:2, :] * shifted(1) + cw_ref[2:3, :] * cu + cb_ref[...]
    gated = (p_ref[:, :d].astype(F32) * y).astype(BF16)
    _ffn_tail([gated[h] for h in _row_halves(rows)], *refs)


def _out_ffn(mixer_in, x2d, wo, gpost, gpre, wgu, wd, gfpost, conv=None):
    m, d = x2d.shape
    row = lambda r: (r, 0)
    if conv is None:
        body, lead_specs, lead_args = _out_ffn_kernel, [pl.BlockSpec((ROW_BLOCK, d), row)], (mixer_in,)
    else:
        conv_w, conv_b, seq = conv
        body = functools.partial(_conv_out_ffn_kernel, blocks_per_seq=seq // ROW_BLOCK)
        lead_specs = [
            pl.BlockSpec((ROW_BLOCK, 3 * d), row),
            pl.BlockSpec((HALO, 3 * d), lambda r: (jnp.maximum(r * (ROW_BLOCK // HALO) - 1, 0), 0)),
            _resident((3, d)),
            _resident((1, d)),
        ]
        lead_args = (mixer_in, mixer_in, conv_w, conv_b.reshape(1, d))
    return pl.pallas_call(
        body,
        grid=(m // ROW_BLOCK,),
        in_specs=lead_specs + [
            pl.BlockSpec((ROW_BLOCK, d), row),
            _resident(wo.shape),
            _resident((1, d)),
            _resident((1, d)),
            _resident(wgu.shape),
            _resident(wd.shape),
            _resident((1, d)),
        ],
        out_specs=pl.BlockSpec((ROW_BLOCK, d), row),
        out_shape=jax.ShapeDtypeStruct((m, d), F32),
        compiler_params=pltpu.CompilerParams(
            dimension_semantics=("arbitrary",), vmem_limit_bytes=VMEM_LIMIT),
        name="out_ffn",
    )(*lead_args, x2d, wo, gpost.reshape(1, d), gpre.reshape(1, d), wgu, wd, gfpost.reshape(1, d))


SB_PAIRS = SB_HEADS // 2
SB_SKIP_MASS = 88.0


def _sb_kernel(q_ref, k_ref, v_ref, tri_ref, o_ref, q2_ref, carry_ref, acc_ref, least_ref):
    t = SB_BLOCK
    qi = pl.program_id(1)
    pairs = range(SB_PAIRS)
    cols = [slice(hp * LANES, (hp + 1) * LANES) for hp in pairs]
    lane_lo = lax.broadcasted_iota(jnp.int32, (t, LANES), 1) < SB_HEAD_DIM
    row = lax.broadcasted_iota(jnp.int32, (2 * t, t), 0)
    col = lax.broadcasted_iota(jnp.int32, (2 * t, t), 1)
    before = col < jnp.where(row >= t, row - t, row)

    for hp in pairs:
        qs = q_ref[0, :, cols[hp]] * jnp.asarray(SB_HEAD_DIM ** -0.5, BF16)
        q2_ref[hp] = jnp.concatenate([jnp.where(lane_lo, qs, 0), jnp.where(lane_lo, 0, qs)], axis=0)

    def sweep(tiles, fresh):
        starts = [pl.multiple_of(ti * t, t) for ti, _ in tiles]
        units = [(hp, k) for hp in pairs for k in range(len(tiles))]
        zs = {(hp, k): _dot_nt(q2_ref[hp], k_ref[0, pl.ds(starts[k], t), cols[hp]]) for hp, k in units}
        sps = {}
        for hp, k in units:
            z = zs[hp, k]
            sp = jnp.maximum(z, 0.0) + jnp.log(1.0 + jnp.exp(-jnp.abs(z)))
            if tiles[k][1] is not None:
                sp = jnp.where(tiles[k][1], sp, 0.0)
            sps[hp, k] = sp.astype(BF16)
        css = {u: _dot(sps[u], tri_ref[...]) for u in units}
        a_cats = []
        least = None
        for hp in pairs:
            carry = None if fresh else carry_ref[hp]
            parts = []
            for k, (_, mask) in enumerate(tiles):
                cs = css[hp, k]
                a = jnp.exp(zs[hp, k] - (cs[:, :t] if carry is None else cs[:, :t] + carry))
                if mask is not None:
                    a = jnp.where(mask, a, 0.0)
                a = a.astype(BF16)
                parts += [a[:t], a[t:]]
                carry = cs[:, t:] if carry is None else carry + cs[:, t:]
            a_cats.append(jnp.concatenate(parts, axis=1))
            carry_ref[hp] = carry
            least = carry if least is None else jnp.minimum(least, carry)
        for hp in pairs:
            vparts = []
            for k in range(len(tiles)):
                vt = v_ref[0, pl.ds(starts[k], t), cols[hp]]
                vparts += [jnp.where(lane_lo, vt, 0), jnp.where(lane_lo, 0, vt)]
            out = _dot(a_cats[hp], jnp.concatenate(vparts, axis=0))
            acc_ref[hp] = out if fresh else acc_ref[hp] + out
        return jnp.min(least)

    @pl.when(qi == 0)
    def _():
        least_ref[0] = sweep([(qi, before)], True)

    @pl.when(qi > 0)
    def _():
        least_ref[0] = sweep([(qi, before), (qi - 1, None)], True)

    lax.while_loop(lambda st: (st[0] <= qi) & (st[1] < SB_SKIP_MASS),
                   lambda st: (st[0] + 1, sweep([(qi - st[0], None)], False)),
                   (jnp.int32(2), least_ref[0]))
    for hp in pairs:
        o_ref[0, :, cols[hp]] = acc_ref[hp].astype(o_ref.dtype)


def _sb_tri():
    t = SB_BLOCK
    j = lax.broadcasted_iota(jnp.int32, (t, 2 * t), 0)
    s = lax.broadcasted_iota(jnp.int32, (t, 2 * t), 1)
    return ((j >= s) | (s >= t)).astype(BF16)


def _sb_attention(qkv):
    b, s, _ = qkv.shape
    t = SB_BLOCK
    return pl.pallas_call(
        _sb_kernel,
        grid=(b, s // t),
        in_specs=[
            pl.BlockSpec((1, t, D_MODEL), lambda bi, qi: (bi, qi, 0)),
            pl.BlockSpec((1, s, D_MODEL), lambda bi, qi: (bi, 0, 1)),
            pl.BlockSpec((1, s, D_MODEL), lambda bi, qi: (bi, 0, 2)),
            pl.BlockSpec((t, 2 * t), lambda bi, qi: (0, 0)),
        ],
        out_specs=pl.BlockSpec((1, t, D_MODEL), lambda bi, qi: (bi, qi, 0)),
        out_shape=jax.ShapeDtypeStruct((b, s, D_MODEL), BF16),
        scratch_shapes=[
            pltpu.VMEM((SB_PAIRS, 2 * t, LANES), BF16),
            pltpu.VMEM((SB_PAIRS, 2 * t, t), F32),
            pltpu.VMEM((SB_PAIRS, t, LANES), F32),
            pltpu.SMEM((1,), F32),
        ],
        compiler_params=pltpu.CompilerParams(
            dimension_semantics=("arbitrary", "arbitrary"), vmem_limit_bytes=VMEM_LIMIT),
        name="sb_attention",
    )(qkv, qkv, qkv, _sb_tri())


def _mlstm_expand():
    r = lax.broadcasted_iota(jnp.int32, (LANES, ML_HEADS * LANES), 0)
    c = lax.broadcasted_iota(jnp.int32, (LANES, ML_HEADS * LANES), 1)
    return (r == ML_HEADS + c // LANES).astype(BF16)


def _mlstm_kernel(p_ref, g_ref, bias_ref, e_ref, o_ref, st_ref, m_ref):
    L = ML_CHUNK
    heads = range(ML_HEADS)
    chunks = range(ML_STEP // L)

    @pl.when(pl.program_id(1) == 0)
    def _():
        st_ref[...] = jnp.zeros_like(st_ref)
        m_ref[...] = jnp.zeros_like(m_ref)

    r = lax.broadcasted_iota(jnp.int32, (L, L), 0)
    c = lax.broadcasted_iota(jnp.int32, (L, L), 1)
    causal = c <= r
    tril = causal.astype(BF16)
    eye = (lax.broadcasted_iota(jnp.int32, (LANES, LANES), 0)
           == lax.broadcasted_iota(jnp.int32, (LANES, LANES), 1)).astype(BF16)
    lane = lax.broadcasted_iota(jnp.int32, (L, LANES), 1)
    lane_lo = lane < ML_QK_DIM
    is_f = (lane >= ML_HEADS) & (lane < 2 * ML_HEADS)
    ones_col = (lane == 0).astype(BF16)

    m_all = m_ref[...]
    gate = []
    for ck in chunks:
        rows = slice(ck * L, (ck + 1) * L)
        g = g_ref[0, rows, :] + bias_ref[...]
        lf = jnp.where(is_f, jnp.minimum(g, 0.0) - jnp.log1p(jnp.exp(-jnp.abs(g))), 0.0)
        b_c = sum(_dot(tril, p) for p in _split3(lf))
        src = jnp.where(lane >= ML_HEADS, b_c, g)
        src_t = sum(_dot_nt(eye, jnp.concatenate([p, jnp.zeros_like(p)], axis=0))
                    for p in _split3(src))[:, :L]
        b_last = b_c[L - 1:L, :]
        log_w = b_last - b_c + pltpu.roll(g, ML_HEADS, axis=1)
        m_new = jnp.maximum(b_last + m_all, jnp.max(log_w, axis=0, keepdims=True))
        w_all = jnp.exp(log_w - m_new)
        decay = jnp.exp(b_last + m_all - m_new)
        stacked = jnp.concatenate([b_c, w_all, jnp.broadcast_to(decay, (4, LANES)),
                                   jnp.broadcast_to(m_all, (4, LANES))], axis=0)
        wide = sum(_dot(p, e_ref[...]) for p in _split3(stacked)[:2])
        gate.append((src_t, wide))
        m_all = m_new
    m_ref[...] = m_all

    units = [(ck, h) for ck in chunks for h in heads]
    qs, ks, vs, s_raw = {}, {}, {}, {}
    for ck, h in units:
        rows = slice(ck * L, (ck + 1) * L)
        hp, odd = h // 2, h % 2
        own = lane_lo != bool(odd)
        qs[ck, h] = jnp.where(own, p_ref[0, rows, hp * LANES:(hp + 1) * LANES], 0)
        kh = jnp.where(own, p_ref[0, rows, ML_QK_WIDTH + hp * LANES:ML_QK_WIDTH + (hp + 1) * LANES], 0)
        ks[ck, h] = kh * jnp.asarray(ML_QK_DIM ** -0.5, BF16)
        vh = p_ref[0, rows, 2 * ML_QK_WIDTH + h * LANES:2 * ML_QK_WIDTH + (h + 1) * LANES]
        vs[ck, h] = jnp.concatenate([vh, ones_col], axis=1)
        s_raw[ck, h] = _dot_nt(qs[ck, h], ks[ck, h])
    s_bf, inter, m_ts = {}, {}, {}
    for ck, h in units:
        src_t, wide = gate[ck]
        hb = slice(h * LANES, (h + 1) * LANES)
        b_col = wide[0:L, hb][:, :L]
        m_inter = b_col[:, :1] + wide[2 * L + 4:2 * L + 5, hb][:, :1]
        c_row = src_t[h:h + 1, :] - src_t[ML_HEADS + h:ML_HEADS + h + 1, :]
        log_d = jnp.where(causal, b_col + c_row, -jnp.inf)
        m_t = jnp.maximum(m_inter, jnp.max(log_d, axis=-1, keepdims=True))
        s_bf[ck, h] = (s_raw[ck, h] * jnp.exp(log_d - m_t)).astype(BF16)
        inter[ck, h] = jnp.exp(m_inter - m_t)
        m_ts[ck, h] = m_t
    sv = {u: _dot(s_bf[u], vs[u]) for u in units}

    for ck in chunks:
        rows = slice(ck * L, (ck + 1) * L)
        wide = gate[ck][1]
        qc = [_dot(qs[ck, h], st_ref[h].astype(BF16)) for h in heads]
        kws = []
        for h in heads:
            hb = slice(h * LANES, (h + 1) * LANES)
            tot = sv[ck, h] + inter[ck, h] * qc[h]
            den = tot[:, LANES:LANES + 1]
            h_out = tot[:, :LANES] / jnp.maximum(jnp.abs(den), jnp.exp(-m_ts[ck, h]))
            oh = p_ref[0, rows, 2 * ML_QK_WIDTH + D_MODEL + h * LANES:
                       2 * ML_QK_WIDTH + D_MODEL + (h + 1) * LANES].astype(F32)
            o_ref[0, rows, hb] = ((1.0 / (1.0 + jnp.exp(-oh))) * h_out).astype(o_ref.dtype)
            kws.append((ks[ck, h].astype(F32) * wide[L:2 * L, hb]).astype(BF16))
        upd = [_dot_tn(kws[h], vs[ck, h]) for h in heads]
        for h in heads:
            d_row = wide[2 * L:2 * L + 1, h * LANES:(h + 1) * LANES]
            st_ref[h] = jnp.concatenate([d_row, d_row], axis=1) * st_ref[h] + upd[h]


def _mlstm_core(main, gates, bias_row):
    b, s, n = main.shape
    return pl.pallas_call(
        _mlstm_kernel,
        grid=(b, s // ML_STEP),
        in_specs=[
            pl.BlockSpec((1, ML_STEP, n), lambda bi, i: (bi, i, 0)),
            pl.BlockSpec((1, ML_STEP, LANES), lambda bi, i: (bi, i, 0)),
            pl.BlockSpec((1, LANES), lambda bi, i: (0, 0)),
            pl.BlockSpec((LANES, ML_HEADS * LANES), lambda bi, i: (0, 0)),
        ],
        out_specs=pl.BlockSpec((1, ML_STEP, D_MODEL), lambda bi, i: (bi, i, 0)),
        out_shape=jax.ShapeDtypeStruct((b, s, D_MODEL), BF16),
        scratch_shapes=[
            pltpu.VMEM((ML_HEADS, LANES, 2 * LANES), F32),
            pltpu.VMEM((1, LANES), F32),
        ],
        compiler_params=pltpu.CompilerParams(dimension_semantics=("arbitrary", "arbitrary")),
        name="mlstm_core",
    )(main, gates, bias_row, _mlstm_expand())


def kernel(x, ln_mix_pre, ln_mix_post, ln_ffn_pre, ln_ffn_post, sb_w_qkv, sb_w_o, gc_w_in, gc_conv_w,
           gc_conv_b, gc_w_out, ml_w_in, ml_b_i, ml_b_f, ml_w_out, ffn_w_gu, ffn_w_down):
    bsz, seq, d = x.shape
    x2d = x.reshape(bsz * seq, d)
    w_in_src = ((sb_w_qkv, 3 * d), (gc_w_in, 3 * d), (ml_w_in, ML_MAIN_WIDTH))
    w_out_src = (sb_w_o, gc_w_out, ml_w_out)
    ml_gate = jnp.pad(ml_w_in[:, :, ML_MAIN_WIDTH:],
                      ((0, 0), (0, 0), (0, LANES - 2 * ML_HEADS))).astype(BF16)
    w_in = sb_w_qkv[:1].astype(BF16)
    for i in range(DEPTH):
        kind, j = i % N_MIXERS, i // N_MIXERS
        projs = [(w_in, 0, BF16)] + ([(ml_gate, j, F32)] if kind == 2 else [])
        cast = [(w_out_src[kind], j), (ffn_w_gu, i), (ffn_w_down, i)]
        if i + 1 < DEPTH:
            nxt, cols = w_in_src[(i + 1) % N_MIXERS]
            cast.append((nxt, (i + 1) // N_MIXERS, cols))
        proj, *rest = _norm_proj(x2d, ln_mix_pre[i], projs, cast=cast)
        wo, wgu, wd = rest[len(projs) - 1:len(projs) + 2]
        if i + 1 < DEPTH:
            w_in = rest[-1][None]
        conv = None
        if kind == 0:
            mixed = _sb_attention(proj.reshape(bsz, seq, 3 * d))
        elif kind == 1:
            mixed = proj
            conv = (gc_conv_w[j], gc_conv_b[j], seq)
        else:
            bias_row = jnp.pad(jnp.concatenate([ml_b_i[j], ml_b_f[j]]), (0, LANES - 2 * ML_HEADS))
            mixed = _mlstm_core(proj.reshape(bsz, seq, ML_MAIN_WIDTH), rest[0].reshape(bsz, seq, LANES),
                                bias_row.reshape(1, LANES))
        x2d = _out_ffn(mixed.reshape(bsz * seq, -1), x2d, wo, ln_mix_post[i], ln_ffn_pre[i],
                       wgu, wd, ln_ffn_post[i], conv)
    return x2d.reshape(bsz, seq, d)
```

```python
import functools

import jax
import jax.numpy as jnp
from jax import lax
from jax.experimental import pallas as pl
from jax.experimental.pallas import tpu as pltpu

D_MODEL = 1024
DEPTH = 4
N_MIXERS = 3
SB_HEADS = 16
SB_HEAD_DIM = 64
ML_HEADS = 8
ML_QK_DIM = 64
ML_V_DIM = 128
ML_QK_WIDTH = ML_HEADS * ML_QK_DIM
ML_MAIN_WIDTH = 2 * ML_QK_WIDTH + 2 * D_MODEL
ML_CHUNK = 64
D_FF = 2816
RMS_EPS = 1e-6

LANES = 128
ROW_BLOCK = 512
SB_BLOCK = 128
ML_STEP = 8 * ML_CHUNK
MXU_TILE = 256
FFN_CHUNKS = ((0, 6 * MXU_TILE), (6 * MXU_TILE, D_FF))
VMEM_LIMIT = 56 * 1024 * 1024

F32 = jnp.float32
BF16 = jnp.bfloat16


def _rms(x, g):
    ms = jnp.mean(x * x, axis=-1, keepdims=True)
    return x * lax.rsqrt(ms + RMS_EPS) * g


def _dot(a, b):
    return jnp.dot(a, b, preferred_element_type=F32)


def _dot_nt(a, b):
    return lax.dot_general(a, b, (((1,), (1,)), ((), ())), preferred_element_type=F32)


def _dot_tn(a, b):
    return lax.dot_general(a, b, (((0,), (0,)), ((), ())), preferred_element_type=F32)


def _split3(x):
    p0 = x.astype(BF16)
    r1 = x - p0.astype(F32)
    p1 = r1.astype(BF16)
    p2 = (r1 - p1.astype(F32)).astype(BF16)
    return p0, p1, p2


def _resident(shape):
    nd = len(shape)
    return pl.BlockSpec(shape, lambda *_: (0,) * nd, pipeline_mode=pl.Buffered(1))


def _resident_layer(stack, j):
    _, r, c = stack.shape
    return pl.BlockSpec((None, r, c), lambda *_: (j, 0, 0), pipeline_mode=pl.Buffered(1))


def _norm_proj_kernel(x_ref, g_ref, *refs, n_proj):
    n_cast = len(refs) // 2 - n_proj
    ws, srcs = refs[:n_proj], refs[n_proj:n_proj + n_cast]
    outs, dsts = refs[n_proj + n_cast:2 * n_proj + n_cast], refs[2 * n_proj + n_cast:]
    hn = _rms(x_ref[...], g_ref[...]).astype(BF16)
    for w_ref, o_ref in zip(ws, outs):
        n = o_ref.shape[-1]
        step = D_MODEL if n % D_MODEL == 0 else n
        for c0 in range(0, n, step):
            o_ref[:, c0:c0 + step] = _dot(hn, w_ref[:, c0:c0 + step]).astype(o_ref.dtype)
    for src, dst in zip(srcs, dsts):
        dst[...] = src[...].astype(dst.dtype)


BF16_SUBLANES = 16
NP_ROW_BLOCK = 2 * ROW_BLOCK


def _norm_proj(x2d, g, projs, cast=()):
    m, d = x2d.shape
    steps = m // NP_ROW_BLOCK
    row = lambda i: (i, 0)
    in_specs = [pl.BlockSpec((NP_ROW_BLOCK, d), row), _resident((1, d))]
    in_specs += [_resident_layer(w, j) for w, j, _ in projs]
    out_specs = [pl.BlockSpec((NP_ROW_BLOCK, w.shape[2]), row) for w, _, _ in projs]
    out_shape = [jax.ShapeDtypeStruct((m, w.shape[2]), dt) for w, _, dt in projs]
    for stack, layer, *cols in cast:
        r, c = stack.shape[1], (cols[0] if cols else stack.shape[2])
        slabs = max(s for s in range(1, steps + 1) if r % s == 0 and (r // s) % BF16_SUBLANES == 0)
        in_specs.append(pl.BlockSpec(
            (None, r // slabs, c), lambda i, layer=layer, slabs=slabs: (layer, jnp.minimum(i, slabs - 1), 0)))
        out_specs.append(pl.BlockSpec((r // slabs, c), lambda i, slabs=slabs: (jnp.minimum(i, slabs - 1), 0)))
        out_shape.append(jax.ShapeDtypeStruct((r, c), BF16))
    return pl.pallas_call(
        functools.partial(_norm_proj_kernel, n_proj=len(projs)),
        grid=(steps,),
        in_specs=in_specs,
        out_specs=out_specs,
        out_shape=out_shape,
        compiler_params=pltpu.CompilerParams(
            dimension_semantics=("arbitrary",), vmem_limit_bytes=VMEM_LIMIT),
        name="norm_proj",
    )(x2d, g.reshape(1, d), *[w for w, _, _ in projs], *[entry[0] for entry in cast])


def _row_halves(rows):
    return [slice(k * rows // 2, (k + 1) * rows // 2) for k in range(2)]


def _ffn_tail(mixer_out, x_ref, wo_ref, gpost_ref, gpre_ref, wgu_ref, wd_ref, gfpost_ref, o_ref):
    halves = _row_halves(o_ref.shape[0])
    mixed = [_dot(mo, wo_ref[...]) for mo in mixer_out]
    x1 = [x_ref[h, :] + _rms(mx, gpost_ref[...]) for h, mx in zip(halves, mixed)]
    hn = [_rms(v, gpre_ref[...]).astype(BF16) for v in x1]
    acc = [None, None]
    for c0, c1 in FFN_CHUNKS:
        g = [_dot(v, wgu_ref[:, c0:c1]) for v in hn]
        u = [_dot(v, wgu_ref[:, D_FF + c0:D_FF + c1]) for v in hn]
        a = [(gk * (1.0 / (1.0 + jnp.exp(-gk))) * uk).astype(BF16) for gk, uk in zip(g, u)]
        part = [_dot(ak, wd_ref[c0:c1, :]) for ak in a]
        acc = [p if c is None else c + p for c, p in zip(acc, part)]
    for h, v, c in zip(halves, x1, acc):
        o_ref[h, :] = v + _rms(c, gfpost_ref[...])


def _out_ffn_kernel(m_ref, *refs):
    _ffn_tail([m_ref[h, :] for h in _row_halves(m_ref.shape[0])], *refs)


HALO = 16


def _conv_out_ffn_kernel(p_ref, h_ref, cw_ref, cb_ref, *refs, blocks_per_seq):
    d = D_MODEL
    rows = p_ref.shape[0]
    cu = p_ref[:, d:2 * d].astype(F32) * p_ref[:, 2 * d:].astype(F32)
    cu_prev = h_ref[:, d:2 * d].astype(F32) * h_ref[:, 2 * d:].astype(F32)
    cu_prev = jnp.where(pl.program_id(0) % blocks_per_seq == 0, 0.0, cu_prev)
    top_row = lax.broadcasted_iota(jnp.int32, (HALO, d), 0)

    def shifted(k):
        top = jnp.where(top_row < k, pltpu.roll(cu_prev, k, axis=0), pltpu.roll(cu[:HALO], k, axis=0))
        return jnp.concatenate([top, pltpu.roll(cu, k, axis=0)[HALO:]], axis=0)

    y = cw_ref[0:1, :] * shifted(2) + cw_ref[1:2, :] * shifted(1) + cw_ref[2:3, :] * cu + cb_ref[...]
    gated = (p_ref[:, :d].astype(F32) * y).astype(BF16)
    _ffn_tail([gated[h] for h in _row_halves(rows)], *refs)


def _out_ffn(mixer_in, x2d, wo, gpost, gpre, wgu, wd, gfpost, conv=None):
    m, d = x2d.shape
    row = lambda r: (r, 0)
    if conv is None:
        body, lead_specs, lead_args = _out_ffn_kernel, [pl.BlockSpec((ROW_BLOCK, d), row)], (mixer_in,)
    else:
        conv_w, conv_b, seq = conv
        body = functools.partial(_conv_out_ffn_kernel, blocks_per_seq=seq // ROW_BLOCK)
        lead_specs = [
            pl.BlockSpec((ROW_BLOCK, 3 * d), row),
            pl.BlockSpec((HALO, 3 * d), lambda r: (jnp.maximum(r * (ROW_BLOCK // HALO) - 1, 0), 0)),
            _resident((3, d)),
            _resident((1, d)),
        ]
        lead_args = (mixer_in, mixer_in, conv_w, conv_b.reshape(1, d))
    return pl.pallas_call(
        body,
        grid=(m // ROW_BLOCK,),
        in_specs=lead_specs + [
            pl.BlockSpec((ROW_BLOCK, d), row),
            _resident(wo.shape),
            _resident((1, d)),
            _resident((1, d)),
            _resident(wgu.shape),
            _resident(wd.shape),
            _resident((1, d)),
        ],
        out_specs=pl.BlockSpec((ROW_BLOCK, d), row),
        out_shape=jax.ShapeDtypeStruct((m, d), F32),
        compiler_params=pltpu.CompilerParams(
            dimension_semantics=("arbitrary",), vmem_limit_bytes=VMEM_LIMIT),
        name="out_ffn",
    )(*lead_args, x2d, wo, gpost.reshape(1, d), gpre.reshape(1, d), wgu, wd, gfpost.reshape(1, d))


SB_PAIRS = SB_HEADS // 2
SB_SKIP_MASS = 88.0
SB_SUB = 4


def _sb_kernel(*refs):
    def query_tile(sub, _):
        rows = pl.ds(pl.multiple_of(sub * SB_BLOCK, SB_BLOCK), SB_BLOCK)
        _sb_query_tile(pl.program_id(1) * SB_SUB + sub, rows, *refs)
        return 0

    lax.fori_loop(0, SB_SUB, query_tile, 0)


def _sb_query_tile(qi, rows, q_ref, k_ref, v_ref, tri_ref, o_ref, q2_ref, carry_ref, acc_ref, least_ref):
    t = SB_BLOCK
    pairs = range(SB_PAIRS)
    cols = [slice(hp * LANES, (hp + 1) * LANES) for hp in pairs]
    lane_lo = lax.broadcasted_iota(jnp.int32, (t, LANES), 1) < SB_HEAD_DIM
    row = lax.broadcasted_iota(jnp.int32, (2 * t, t), 0)
    col = lax.broadcasted_iota(jnp.int32, (2 * t, t), 1)
    before = col < jnp.where(row >= t, row - t, row)

    for hp in pairs:
        qs = q_ref[0, rows, cols[hp]] * jnp.asarray(SB_HEAD_DIM ** -0.5, BF16)
        q2_ref[hp] = jnp.concatenate([jnp.where(lane_lo, qs, 0), jnp.where(lane_lo, 0, qs)], axis=0)

    def sweep(tiles, fresh):
        starts = [pl.multiple_of(ti * t, t) for ti, _ in tiles]
        units = [(hp, k) for hp in pairs for k in range(len(tiles))]
        zs = {(hp, k): _dot_nt(q2_ref[hp], k_ref[0, pl.ds(starts[k], t), cols[hp]]) for hp, k in units}
        sps = {}
        for hp, k in units:
            z = zs[hp, k]
            sp = jnp.maximum(z, 0.0) + jnp.log(1.0 + jnp.exp(-jnp.abs(z)))
            if tiles[k][1] is not None:
                sp = jnp.where(tiles[k][1], sp, 0.0)
            sps[hp, k] = sp.astype(BF16)
        css = {u: _dot(sps[u], tri_ref[...]) for u in units}
        a_cats = []
        least = None
        for hp in pairs:
            carry = None if fresh else carry_ref[hp]
            parts = []
            for k, (_, mask) in enumerate(tiles):
                cs = css[hp, k]
                a = jnp.exp(zs[hp, k] - (cs[:, :t] if carry is None else cs[:, :t] + carry))
                if mask is not None:
                    a = jnp.where(mask, a, 0.0)
                a = a.astype(BF16)
                parts += [a[:t], a[t:]]
                carry = cs[:, t:] if carry is None else carry + cs[:, t:]
            a_cats.append(jnp.concatenate(parts, axis=1))
            carry_ref[hp] = carry
            least = carry if least is None else jnp.minimum(least, carry)
        for hp in pairs:
            vparts = []
            for k in range(len(tiles)):
                vt = v_ref[0, pl.ds(starts[k], t), cols[hp]]
                vparts += [jnp.where(lane_lo, vt, 0), jnp.where(lane_lo, 0, vt)]
            out = _dot(a_cats[hp], jnp.concatenate(vparts, axis=0))
            acc_ref[hp] = out if fresh else acc_ref[hp] + out
        return jnp.min(least)

    @pl.when(qi == 0)
    def _():
        least_ref[0] = sweep([(qi, before)], True)

    @pl.when(qi > 0)
    def _():
        least_ref[0] = sweep([(qi, before), (qi - 1, None)], True)

    lax.while_loop(lambda st: (st[0] <= qi) & (st[1] < SB_SKIP_MASS),
                   lambda st: (st[0] + 1, sweep([(qi - st[0], None)], False)),
                   (jnp.int32(2), least_ref[0]))
    for hp in pairs:
        o_ref[0, rows, cols[hp]] = acc_ref[hp].astype(o_ref.dtype)


def _sb_tri():
    t = SB_BLOCK
    j = lax.broadcasted_iota(jnp.int32, (t, 2 * t), 0)
    s = lax.broadcasted_iota(jnp.int32, (t, 2 * t), 1)
    return ((j >= s) | (s >= t)).astype(BF16)


def _sb_attention(qkv):
    b, s, _ = qkv.shape
    t = SB_BLOCK
    return pl.pallas_call(
        _sb_kernel,
        grid=(b, s // (SB_SUB * t)),
        in_specs=[
            pl.BlockSpec((1, SB_SUB * t, D_MODEL), lambda bi, qi: (bi, qi, 0)),
            pl.BlockSpec((1, s, D_MODEL), lambda bi, qi: (bi, 0, 1)),
            pl.BlockSpec((1, s, D_MODEL), lambda bi, qi: (bi, 0, 2)),
            pl.BlockSpec((t, 2 * t), lambda bi, qi: (0, 0)),
        ],
        out_specs=pl.BlockSpec((1, SB_SUB * t, D_MODEL), lambda bi, qi: (bi, qi, 0)),
        out_shape=jax.ShapeDtypeStruct((b, s, D_MODEL), BF16),
        scratch_shapes=[
            pltpu.VMEM((SB_PAIRS, 2 * t, LANES), BF16),
            pltpu.VMEM((SB_PAIRS, 2 * t, t), F32),
            pltpu.VMEM((SB_PAIRS, t, LANES), F32),
            pltpu.SMEM((1,), F32),
        ],
        compiler_params=pltpu.CompilerParams(
            dimension_semantics=("arbitrary", "arbitrary"), vmem_limit_bytes=VMEM_LIMIT),
        name="sb_attention",
    )(qkv, qkv, qkv, _sb_tri())


def _mlstm_expand():
    r = lax.broadcasted_iota(jnp.int32, (LANES, ML_HEADS * LANES), 0)
    c = lax.broadcasted_iota(jnp.int32, (LANES, ML_HEADS * LANES), 1)
    return (r == ML_HEADS + c // LANES).astype(BF16)


def _mlstm_kernel(p_ref, g_ref, bias_ref, e_ref, o_ref, st_ref, m_ref):
    L = ML_CHUNK
    heads = range(ML_HEADS)
    chunks = range(ML_STEP // L)

    @pl.when(pl.program_id(1) == 0)
    def _():
        st_ref[...] = jnp.zeros_like(st_ref)
        m_ref[...] = jnp.zeros_like(m_ref)

    r = lax.broadcasted_iota(jnp.int32, (L, L), 0)
    c = lax.broadcasted_iota(jnp.int32, (L, L), 1)
    causal = c <= r
    tril = causal.astype(BF16)
    eye = (lax.broadcasted_iota(jnp.int32, (LANES, LANES), 0)
           == lax.broadcasted_iota(jnp.int32, (LANES, LANES), 1)).astype(BF16)
    lane = lax.broadcasted_iota(jnp.int32, (L, LANES), 1)
    lane_lo = lane < ML_QK_DIM
    is_f = (lane >= ML_HEADS) & (lane < 2 * ML_HEADS)
    ones_col = (lane == 0).astype(BF16)

    m_all = m_ref[...]
    gate = []
    for ck in chunks:
        rows = slice(ck * L, (ck + 1) * L)
        g = g_ref[0, rows, :] + bias_ref[...]
        lf = jnp.where(is_f, jnp.minimum(g, 0.0) - jnp.log1p(jnp.exp(-jnp.abs(g))), 0.0)
        b_c = sum(_dot(tril, p) for p in _split3(lf))
        src = jnp.where(lane >= ML_HEADS, b_c, g)
        src_t = sum(_dot_nt(eye, jnp.concatenate([p, jnp.zeros_like(p)], axis=0))
                    for p in _split3(src))[:, :L]
        b_last = b_c[L - 1:L, :]
        log_w = b_last - b_c + pltpu.roll(g, ML_HEADS, axis=1)
        m_new = jnp.maximum(b_last + m_all, jnp.max(log_w, axis=0, keepdims=True))
        w_all = jnp.exp(log_w - m_new)
        decay = jnp.exp(b_last + m_all - m_new)
        stacked = jnp.concatenate([b_c, w_all, jnp.broadcast_to(decay, (4, LANES)),
                                   jnp.broadcast_to(m_all, (4, LANES))], axis=0)
        wide = sum(_dot(p, e_ref[...]) for p in _split3(stacked)[:2])
        gate.append((src_t, wide))
        m_all = m_new
    m_ref[...] = m_all

    units = [(ck, h) for ck in chunks for h in heads]
    qs, ks, vs, s_raw = {}, {}, {}, {}
    for ck, h in units:
        rows = slice(ck * L, (ck + 1) * L)
        hp, odd = h // 2, h % 2
        own = lane_lo != bool(odd)
        qs[ck, h] = jnp.where(own, p_ref[0, rows, hp * LANES:(hp + 1) * LANES], 0)
        kh = jnp.where(own, p_ref[0, rows, ML_QK_WIDTH + hp * LANES:ML_QK_WIDTH + (hp + 1) * LANES], 0)
        ks[ck, h] = kh * jnp.asarray(ML_QK_DIM ** -0.5, BF16)
        vh = p_ref[0, rows, 2 * ML_QK_WIDTH + h * LANES:2 * ML_QK_WIDTH + (h + 1) * LANES]
        vs[ck, h] = jnp.concatenate([vh, ones_col], axis=1)
        s_raw[ck, h] = _dot_nt(qs[ck, h], ks[ck, h])
    s_bf, inter, m_ts = {}, {}, {}
    for ck, h in units:
        src_t, wide = gate[ck]
        hb = slice(h * LANES, (h + 1) * LANES)
        b_col = wide[0:L, hb][:, :L]
        m_inter = b_col[:, :1] + wide[2 * L + 4:2 * L + 5, hb][:, :1]
        c_row = src_t[h:h + 1, :] - src_t[ML_HEADS + h:ML_HEADS + h + 1, :]
        log_d = jnp.where(causal, b_col + c_row, -jnp.inf)
        m_t = jnp.maximum(m_inter, jnp.max(log_d, axis=-1, keepdims=True))
        s_bf[ck, h] = (s_raw[ck, h] * jnp.exp(log_d - m_t)).astype(BF16)
        inter[ck, h] = jnp.exp(m_inter - m_t)
        m_ts[ck, h] = m_t
    sv = {u: _dot(s_bf[u], vs[u]) for u in units}

    for ck in chunks:
        rows = slice(ck * L, (ck + 1) * L)
        wide = gate[ck][1]
        qc = [_dot(qs[ck, h], st_ref[h].astype(BF16)) for h in heads]
        kws = []
        for h in heads:
            hb = slice(h * LANES, (h + 1) * LANES)
            tot = sv[ck, h] + inter[ck, h] * qc[h]
            den = tot[:, LANES:LANES + 1]
            h_out = tot[:, :LANES] / jnp.maximum(jnp.abs(den), jnp.exp(-m_ts[ck, h]))
            oh = p_ref[0, rows, 2 * ML_QK_WIDTH + D_MODEL + h * LANES:
                       2 * ML_QK_WIDTH + D_MODEL + (h + 1) * LANES].astype(F32)
            o_ref[0, rows, hb] = ((1.0 / (1.0 + jnp.exp(-oh))) * h_out).astype(o_ref.dtype)
            kws.append((ks[ck, h].astype(F32) * wide[L:2 * L, hb]).astype(BF16))
        upd = [_dot_tn(kws[h], vs[ck, h]) for h in heads]
        for h in heads:
            d_row = wide[2 * L:2 * L + 1, h * LANES:(h + 1) * LANES]
            st_ref[h] = jnp.concatenate([d_row, d_row], axis=1) * st_ref[h] + upd[h]


def _mlstm_core(main, gates, bias_row):
    b, s, n = main.shape
    return pl.pallas_call(
        _mlstm_kernel,
        grid=(b, s // ML_STEP),
        in_specs=[
            pl.BlockSpec((1, ML_STEP, n), lambda bi, i: (bi, i, 0)),
            pl.BlockSpec((1, ML_STEP, LANES), lambda bi, i: (bi, i, 0)),
            pl.BlockSpec((1, LANES), lambda bi, i: (0, 0)),
            pl.BlockSpec((LANES, ML_HEADS * LANES), lambda bi, i: (0, 0)),
        ],
        out_specs=pl.BlockSpec((1, ML_STEP, D_MODEL), lambda bi, i: (bi, i, 0)),
        out_shape=jax.ShapeDtypeStruct((b, s, D_MODEL), BF16),
        scratch_shapes=[
            pltpu.VMEM((ML_HEADS, LANES, 2 * LANES), F32),
            pltpu.VMEM((1, LANES), F32),
        ],
        compiler_params=pltpu.CompilerParams(dimension_semantics=("arbitrary", "arbitrary")),
        name="mlstm_core",
    )(main, gates, bias_row, _mlstm_expand())


def kernel(x, ln_mix_pre, ln_mix_post, ln_ffn_pre, ln_ffn_post, sb_w_qkv, sb_w_o, gc_w_in, gc_conv_w,
           gc_conv_b, gc_w_out, ml_w_in, ml_b_i, ml_b_f, ml_w_out, ffn_w_gu, ffn_w_down):
    bsz, seq, d = x.shape
    x2d = x.reshape(bsz * seq, d)
    w_in_src = ((sb_w_qkv, 3 * d), (gc_w_in, 3 * d), (ml_w_in, ML_MAIN_WIDTH))
    w_out_src = (sb_w_o, gc_w_out, ml_w_out)
    ml_gate = jnp.pad(ml_w_in[:, :, ML_MAIN_WIDTH:],
                      ((0, 0), (0, 0), (0, LANES - 2 * ML_HEADS))).astype(BF16)
    w_in = sb_w_qkv[:1].astype(BF16)
    for i in range(DEPTH):
        kind, j = i % N_MIXERS, i // N_MIXERS
        projs = [(w_in, 0, BF16)] + ([(ml_gate, j, F32)] if kind == 2 else [])
        cast = [(w_out_src[kind], j), (ffn_w_gu, i), (ffn_w_down, i)]
        if i + 1 < DEPTH:
            nxt, cols = w_in_src[(i + 1) % N_MIXERS]
            cast.append((nxt, (i + 1) // N_MIXERS, cols))
        proj, *rest = _norm_proj(x2d, ln_mix_pre[i], projs, cast=cast)
        wo, wgu, wd = rest[len(projs) - 1:len(projs) + 2]
        if i + 1 < DEPTH:
            w_in = rest[-1][None]
        conv = None
        if kind == 0:
            mixed = _sb_attention(proj.reshape(bsz, seq, 3 * d))
        elif kind == 1:
            mixed = proj
            conv = (gc_conv_w[j], gc_conv_b[j], seq)
        else:
            bias_row = jnp.pad(jnp.concatenate([ml_b_i[j], ml_b_f[j]]), (0, LANES - 2 * ML_HEADS))
            mixed = _mlstm_core(proj.reshape(bsz, seq, ML_MAIN_WIDTH), rest[0].reshape(bsz, seq, LANES),
                                bias_row.reshape(1, LANES))
        x2d = _out_ffn(mixed.reshape(bsz * seq, -1), x2d, wo, ln_mix_post[i], ln_ffn_pre[i],
                       wgu, wd, ln_ffn_post[i], conv)
    return x2d.reshape(bsz, seq, d)
```

```python
import functools

import jax
import jax.numpy as jnp
from jax import lax
from jax.experimental import pallas as pl
from jax.experimental.pallas import tpu as pltpu

D_MODEL = 1024
DEPTH = 4
N_MIXERS = 3
SB_HEADS = 16
SB_HEAD_DIM = 64
ML_HEADS = 8
ML_QK_DIM = 64
ML_V_DIM = 128
ML_QK_WIDTH = ML_HEADS * ML_QK_DIM
ML_MAIN_WIDTH = 2 * ML_QK_WIDTH + 2 * D_MODEL
ML_CHUNK = 64
D_FF = 2816
RMS_EPS = 1e-6

LANES = 128
ROW_BLOCK = 512
SB_BLOCK = 128
ML_STEP = 8 * ML_CHUNK
MXU_TILE = 256
FFN_CHUNKS = ((0, 6 * MXU_TILE), (6 * MXU_TILE, D_FF))
VMEM_LIMIT = 56 * 1024 * 1024

F32 = jnp.float32
BF16 = jnp.bfloat16


def _rms(x, g):
    ms = jnp.mean(x * x, axis=-1, keepdims=True)
    return x * lax.rsqrt(ms + RMS_EPS) * g


def _dot(a, b):
    return jnp.dot(a, b, preferred_element_type=F32)


def _dot_nt(a, b):
    return lax.dot_general(a, b, (((1,), (1,)), ((), ())), preferred_element_type=F32)


def _dot_tn(a, b):
    return lax.dot_general(a, b, (((0,), (0,)), ((), ())), preferred_element_type=F32)


def _split3(x):
    p0 = x.astype(BF16)
    r1 = x - p0.astype(F32)
    p1 = r1.astype(BF16)
    p2 = (r1 - p1.astype(F32)).astype(BF16)
    return p0, p1, p2


def _resident(shape):
    nd = len(shape)
    return pl.BlockSpec(shape, lambda *_: (0,) * nd, pipeline_mode=pl.Buffered(1))


def _resident_layer(stack, j):
    _, r, c = stack.shape
    return pl.BlockSpec((None, r, c), lambda *_: (j, 0, 0), pipeline_mode=pl.Buffered(1))


def _norm_proj_kernel(x_ref, g_ref, *refs, n_proj):
    n_cast = len(refs) // 2 - n_proj
    ws, srcs = refs[:n_proj], refs[n_proj:n_proj + n_cast]
    outs, dsts = refs[n_proj + n_cast:2 * n_proj + n_cast], refs[2 * n_proj + n_cast:]
    hn = _rms(x_ref[...], g_ref[...]).astype(BF16)
    for w_ref, o_ref in zip(ws, outs):
        n = o_ref.shape[-1]
        step = D_MODEL if n % D_MODEL == 0 else n
        for c0 in range(0, n, step):
            o_ref[:, c0:c0 + step] = _dot(hn, w_ref[:, c0:c0 + step]).astype(o_ref.dtype)
    for src, dst in zip(srcs, dsts):
        dst[...] = src[...].astype(dst.dtype)


BF16_SUBLANES = 16
NP_ROW_BLOCK = 2 * ROW_BLOCK


def _norm_proj(x2d, g, projs, cast=()):
    m, d = x2d.shape
    steps = m // NP_ROW_BLOCK
    row = lambda i: (i, 0)
    in_specs = [pl.BlockSpec((NP_ROW_BLOCK, d), row), _resident((1, d))]
    in_specs += [_resident_layer(w, j) for w, j, _ in projs]
    out_specs = [pl.BlockSpec((NP_ROW_BLOCK, w.shape[2]), row) for w, _, _ in projs]
    out_shape = [jax.ShapeDtypeStruct((m, w.shape[2]), dt) for w, _, dt in projs]
    for stack, layer, *cols in cast:
        r, c = stack.shape[1], (cols[0] if cols else stack.shape[2])
        slabs = max(s for s in range(1, steps + 1) if r % s == 0 and (r // s) % BF16_SUBLANES == 0)
        in_specs.append(pl.BlockSpec(
            (None, r // slabs, c), lambda i, layer=layer, slabs=slabs: (layer, jnp.minimum(i, slabs - 1), 0)))
        out_specs.append(pl.BlockSpec((r // slabs, c), lambda i, slabs=slabs: (jnp.minimum(i, slabs - 1), 0)))
        out_shape.append(jax.ShapeDtypeStruct((r, c), BF16))
    return pl.pallas_call(
        functools.partial(_norm_proj_kernel, n_proj=len(projs)),
        grid=(steps,),
        in_specs=in_specs,
        out_specs=out_specs,
        out_shape=out_shape,
        compiler_params=pltpu.CompilerParams(
            dimension_semantics=("arbitrary",), vmem_limit_bytes=VMEM_LIMIT),
        name="norm_proj",
    )(x2d, g.reshape(1, d), *[w for w, _, _ in projs], *[entry[0] for entry in cast])


def _row_halves(rows):
    return [slice(k * rows // 2, (k + 1) * rows // 2) for k in range(2)]


def _ffn_tail(mixer_out, x_ref, wo_ref, gpost_ref, gpre_ref, wgu_ref, wd_ref, gfpost_ref, o_ref):
    halves = _row_halves(o_ref.shape[0])
    mixed = [_dot(mo, wo_ref[...]) for mo in mixer_out]
    x1 = [x_ref[h, :] + _rms(mx, gpost_ref[...]) for h, mx in zip(halves, mixed)]
    hn = [_rms(v, gpre_ref[...]).astype(BF16) for v in x1]
    acc = [None, None]
    for c0, c1 in FFN_CHUNKS:
        g = [_dot(v, wgu_ref[:, c0:c1]) for v in hn]
        u = [_dot(v, wgu_ref[:, D_FF + c0:D_FF + c1]) for v in hn]
        a = [(gk * (1.0 / (1.0 + jnp.exp(-gk))) * uk).astype(BF16) for gk, uk in zip(g, u)]
        part = [_dot(ak, wd_ref[c0:c1, :]) for ak in a]
        acc = [p if c is None else c + p for c, p in zip(acc, part)]
    for h, v, c in zip(halves, x1, acc):
        o_ref[h, :] = v + _rms(c, gfpost_ref[...])


def _out_ffn_kernel(m_ref, *refs):
    _ffn_tail([m_ref[h, :] for h in _row_halves(m_ref.shape[0])], *refs)


HALO = 16


def _conv_out_ffn_kernel(p_ref, h_ref, cw_ref, cb_ref, *refs, blocks_per_seq):
    d = D_MODEL
    rows = p_ref.shape[0]
    cu = p_ref[:, d:2 * d].astype(F32) * p_ref[:, 2 * d:].astype(F32)
    cu_prev = h_ref[:, d:2 * d].astype(F32) * h_ref[:, 2 * d:].astype(F32)
    cu_prev = jnp.where(pl.program_id(0) % blocks_per_seq == 0, 0.0, cu_prev)
    top_row = lax.broadcasted_iota(jnp.int32, (HALO, d), 0)

    def shifted(k):
        top = jnp.where(top_row < k, pltpu.roll(cu_prev, k, axis=0), pltpu.roll(cu[:HALO], k, axis=0))
        return jnp.concatenate([top, pltpu.roll(cu, k, axis=0)[HALO:]], axis=0)

    y = cw_ref[0:1, :] * shifted(2) + cw_ref[1:2, :] * shifted(1) + cw_ref[2:3, :] * cu + cb_ref[...]
    gated = (p_ref[:, :d].astype(F32) * y).astype(BF16)
    _ffn_tail([gated[h] for h in _row_halves(rows)], *refs)


def _out_ffn(mixer_in, x2d, wo, gpost, gpre, wgu, wd, gfpost, conv=None):
    m, d = x2d.shape
    row = lambda r: (r, 0)
    if conv is None:
        body, lead_specs, lead_args = _out_ffn_kernel, [pl.BlockSpec((ROW_BLOCK, d), row)], (mixer_in,)
    else:
        conv_w, conv_b, seq = conv
        body = functools.partial(_conv_out_ffn_kernel, blocks_per_seq=seq // ROW_BLOCK)
        lead_specs = [
            pl.BlockSpec((ROW_BLOCK, 3 * d), row),
            pl.BlockSpec((HALO, 3 * d), lambda r: (jnp.maximum(r * (ROW_BLOCK // HALO) - 1, 0), 0)),
            _resident((3, d)),
            _resident((1, d)),
        ]
        lead_args = (mixer_in, mixer_in, conv_w, conv_b.reshape(1, d))
    return pl.pallas_call(
        body,
        grid=(m // ROW_BLOCK,),
        in_specs=lead_specs + [
            pl.BlockSpec((ROW_BLOCK, d), row),
            _resident(wo.shape),
            _resident((1, d)),
            _resident((1, d)),
            _resident(wgu.shape),
            _resident(wd.shape),
            _resident((1, d)),
        ],
        out_specs=pl.BlockSpec((ROW_BLOCK, d), row),
        out_shape=jax.ShapeDtypeStruct((m, d), F32),
        compiler_params=pltpu.CompilerParams(
            dimension_semantics=("arbitrary",), vmem_limit_bytes=VMEM_LIMIT),
        name="out_ffn",
    )(*lead_args, x2d, wo, gpost.reshape(1, d), gpre.reshape(1, d), wgu, wd, gfpost.reshape(1, d))


SB_PAIRS = SB_HEADS // 2
SB_SKIP_MASS = 88.0
SB_SUB = 16


def _sb_kernel(*refs):
    def query_tile(sub, _):
        rows = pl.ds(pl.multiple_of(sub * SB_BLOCK, SB_BLOCK), SB_BLOCK)
        _sb_query_tile(pl.program_id(1) * SB_SUB + sub, rows, *refs)
        return 0

    lax.fori_loop(0, SB_SUB, query_tile, 0)


def _sb_query_tile(qi, rows, q_ref, k_ref, v_ref, tri_ref, o_ref, q2_ref, carry_ref, acc_ref, least_ref):
    t = SB_BLOCK
    pairs = range(SB_PAIRS)
    cols = [slice(hp * LANES, (hp + 1) * LANES) for hp in pairs]
    lane_lo = lax.broadcasted_iota(jnp.int32, (t, LANES), 1) < SB_HEAD_DIM
    row = lax.broadcasted_iota(jnp.int32, (2 * t, t), 0)
    col = lax.broadcasted_iota(jnp.int32, (2 * t, t), 1)
    before = col < jnp.where(row >= t, row - t, row)

    for hp in pairs:
        qs = q_ref[0, rows, cols[hp]] * jnp.asarray(SB_HEAD_DIM ** -0.5, BF16)
        q2_ref[hp] = jnp.concatenate([jnp.where(lane_lo, qs, 0), jnp.where(lane_lo, 0, qs)], axis=0)

    def sweep(tiles, fresh):
        starts = [pl.multiple_of(ti * t, t) for ti, _ in tiles]
        units = [(hp, k) for hp in pairs for k in range(len(tiles))]
        zs = {(hp, k): _dot_nt(q2_ref[hp], k_ref[0, pl.ds(starts[k], t), cols[hp]]) for hp, k in units}
        sps = {}
        for hp, k in units:
            z = zs[hp, k]
            sp = jnp.maximum(z, 0.0) + jnp.log(1.0 + jnp.exp(-jnp.abs(z)))
            if tiles[k][1] is not None:
                sp = jnp.where(tiles[k][1], sp, 0.0)
            sps[hp, k] = sp.astype(BF16)
        css = {u: _dot(sps[u], tri_ref[...]) for u in units}
        a_cats = []
        least = None
        for hp in pairs:
            carry = None if fresh else carry_ref[hp]
            parts = []
            for k, (_, mask) in enumerate(tiles):
                cs = css[hp, k]
                a = jnp.exp(zs[hp, k] - (cs[:, :t] if carry is None else cs[:, :t] + carry))
                if mask is not None:
                    a = jnp.where(mask, a, 0.0)
                a = a.astype(BF16)
                parts += [a[:t], a[t:]]
                carry = cs[:, t:] if carry is None else carry + cs[:, t:]
            a_cats.append(jnp.concatenate(parts, axis=1))
            carry_ref[hp] = carry
            least = carry if least is None else jnp.minimum(least, carry)
        for hp in pairs:
            vparts = []
            for k in range(len(tiles)):
                vt = v_ref[0, pl.ds(starts[k], t), cols[hp]]
                vparts += [jnp.where(lane_lo, vt, 0), jnp.where(lane_lo, 0, vt)]
            out = _dot(a_cats[hp], jnp.concatenate(vparts, axis=0))
            acc_ref[hp] = out if fresh else acc_ref[hp] + out
        return jnp.min(least)

    @pl.when(qi == 0)
    def _():
        least_ref[0] = sweep([(qi, before)], True)

    @pl.when(qi > 0)
    def _():
        least_ref[0] = sweep([(qi, before), (qi - 1, None)], True)

    lax.while_loop(lambda st: (st[0] <= qi) & (st[1] < SB_SKIP_MASS),
                   lambda st: (st[0] + 1, sweep([(qi - st[0], None)], False)),
                   (jnp.int32(2), least_ref[0]))
    for hp in pairs:
        o_ref[0, rows, cols[hp]] = acc_ref[hp].astype(o_ref.dtype)


def _sb_tri():
    t = SB_BLOCK
    j = lax.broadcasted_iota(jnp.int32, (t, 2 * t), 0)
    s = lax.broadcasted_iota(jnp.int32, (t, 2 * t), 1)
    return ((j >= s) | (s >= t)).astype(BF16)


def _sb_attention(qkv):
    b, s, _ = qkv.shape
    t = SB_BLOCK
    return pl.pallas_call(
        _sb_kernel,
        grid=(b, s // (SB_SUB * t)),
        in_specs=[
            pl.BlockSpec((1, SB_SUB * t, D_MODEL), lambda bi, qi: (bi, qi, 0)),
            pl.BlockSpec((1, s, D_MODEL), lambda bi, qi: (bi, 0, 1)),
            pl.BlockSpec((1, s, D_MODEL), lambda bi, qi: (bi, 0, 2)),
            pl.BlockSpec((t, 2 * t), lambda bi, qi: (0, 0)),
        ],
        out_specs=pl.BlockSpec((1, SB_SUB * t, D_MODEL), lambda bi, qi: (bi, qi, 0)),
        out_shape=jax.ShapeDtypeStruct((b, s, D_MODEL), BF16),
        scratch_shapes=[
            pltpu.VMEM((SB_PAIRS, 2 * t, LANES), BF16),
            pltpu.VMEM((SB_PAIRS, 2 * t, t), F32),
            pltpu.VMEM((SB_PAIRS, t, LANES), F32),
            pltpu.SMEM((1,), F32),
        ],
        compiler_params=pltpu.CompilerParams(
            dimension_semantics=("arbitrary", "arbitrary"), vmem_limit_bytes=VMEM_LIMIT),
        name="sb_attention",
    )(qkv, qkv, qkv, _sb_tri())


def _mlstm_expand():
    r = lax.broadcasted_iota(jnp.int32, (LANES, ML_HEADS * LANES), 0)
    c = lax.broadcasted_iota(jnp.int32, (LANES, ML_HEADS * LANES), 1)
    return (r == ML_HEADS + c // LANES).astype(BF16)


def _mlstm_kernel(p_ref, g_ref, bias_ref, e_ref, o_ref, st_ref, m_ref):
    L = ML_CHUNK
    heads = range(ML_HEADS)
    chunks = range(ML_STEP // L)

    @pl.when(pl.program_id(1) == 0)
    def _():
        st_ref[...] = jnp.zeros_like(st_ref)
        m_ref[...] = jnp.zeros_like(m_ref)

    r = lax.broadcasted_iota(jnp.int32, (L, L), 0)
    c = lax.broadcasted_iota(jnp.int32, (L, L), 1)
    causal = c <= r
    tril = causal.astype(BF16)
    eye = (lax.broadcasted_iota(jnp.int32, (LANES, LANES), 0)
           == lax.broadcasted_iota(jnp.int32, (LANES, LANES), 1)).astype(BF16)
    lane = lax.broadcasted_iota(jnp.int32, (L, LANES), 1)
    lane_lo = lane < ML_QK_DIM
    is_f = (lane >= ML_HEADS) & (lane < 2 * ML_HEADS)
    ones_col = (lane == 0).astype(BF16)

    m_all = m_ref[...]
    gate = []
    for ck in chunks:
        rows = slice(ck * L, (ck + 1) * L)
        g = g_ref[0, rows, :] + bias_ref[...]
        lf = jnp.where(is_f, jnp.minimum(g, 0.0) - jnp.log1p(jnp.exp(-jnp.abs(g))), 0.0)
        b_c = sum(_dot(tril, p) for p in _split3(lf))
        src = jnp.where(lane >= ML_HEADS, b_c, g)
        src_t = sum(_dot_nt(eye, jnp.concatenate([p, jnp.zeros_like(p)], axis=0))
                    for p in _split3(src))[:, :L]
        b_last = b_c[L - 1:L, :]
        log_w = b_last - b_c + pltpu.roll(g, ML_HEADS, axis=1)
        m_new = jnp.maximum(b_last + m_all, jnp.max(log_w, axis=0, keepdims=True))
        w_all = jnp.exp(log_w - m_new)
        decay = jnp.exp(b_last + m_all - m_new)
        stacked = jnp.concatenate([b_c, w_all, jnp.broadcast_to(decay, (4, LANES)),
                                   jnp.broadcast_to(m_all, (4, LANES))], axis=0)
        wide = sum(_dot(p, e_ref[...]) for p in _split3(stacked)[:2])
        gate.append((src_t, wide))
        m_all = m_new
    m_ref[...] = m_all

    units = [(ck, h) for ck in chunks for h in heads]
    qs, ks, vs, s_raw = {}, {}, {}, {}
    for ck, h in units:
        rows = slice(ck * L, (ck + 1) * L)
        hp, odd = h // 2, h % 2
        own = lane_lo != bool(odd)
        qs[ck, h] = jnp.where(own, p_ref[0, rows, hp * LANES:(hp + 1) * LANES], 0)
        kh = jnp.where(own, p_ref[0, rows, ML_QK_WIDTH + hp * LANES:ML_QK_WIDTH + (hp + 1) * LANES], 0)
        ks[ck, h] = kh * jnp.asarray(ML_QK_DIM ** -0.5, BF16)
        vh = p_ref[0, rows, 2 * ML_QK_WIDTH + h * LANES:2 * ML_QK_WIDTH + (h + 1) * LANES]
        vs[ck, h] = jnp.concatenate([vh, ones_col], axis=1)
        s_raw[ck, h] = _dot_nt(qs[ck, h], ks[ck, h])
    s_bf, inter, m_ts = {}, {}, {}
    for ck, h in units:
        src_t, wide = gate[ck]
        hb = slice(h * LANES, (h + 1) * LANES)
        b_col = wide[0:L, hb][:, :L]
        m_inter = b_col[:, :1] + wide[2 * L + 4:2 * L + 5, hb][:, :1]
        c_row = src_t[h:h + 1, :] - src_t[ML_HEADS + h:ML_HEADS + h + 1, :]
        log_d = jnp.where(causal, b_col + c_row, -jnp.inf)
        m_t = jnp.maximum(m_inter, jnp.max(log_d, axis=-1, keepdims=True))
        s_bf[ck, h] = (s_raw[ck, h] * jnp.exp(log_d - m_t)).astype(BF16)
        inter[ck, h] = jnp.exp(m_inter - m_t)
        m_ts[ck, h] = m_t
    sv = {u: _dot(s_bf[u], vs[u]) for u in units}

    for ck in chunks:
        rows = slice(ck * L, (ck + 1) * L)
        wide = gate[ck][1]
        qc = [_dot(qs[ck, h], st_ref[h].astype(BF16)) for h in heads]
        kws = []
        for h in heads:
            hb = slice(h * LANES, (h + 1) * LANES)
            tot = sv[ck, h] + inter[ck, h] * qc[h]
            den = tot[:, LANES:LANES + 1]
            h_out = tot[:, :LANES] / jnp.maximum(jnp.abs(den), jnp.exp(-m_ts[ck, h]))
            oh = p_ref[0, rows, 2 * ML_QK_WIDTH + D_MODEL + h * LANES:
                       2 * ML_QK_WIDTH + D_MODEL + (h + 1) * LANES].astype(F32)
            o_ref[0, rows, hb] = ((1.0 / (1.0 + jnp.exp(-oh))) * h_out).astype(o_ref.dtype)
            kws.append((ks[ck, h].astype(F32) * wide[L:2 * L, hb]).astype(BF16))
        upd = [_dot_tn(kws[h], vs[ck, h]) for h in heads]
        for h in heads:
            d_row = wide[2 * L:2 * L + 1, h * LANES:(h + 1) * LANES]
            st_ref[h] = jnp.concatenate([d_row, d_row], axis=1) * st_ref[h] + upd[h]


def _mlstm_core(main, gates, bias_row):
    b, s, n = main.shape
    return pl.pallas_call(
        _mlstm_kernel,
        grid=(b, s // ML_STEP),
        in_specs=[
            pl.BlockSpec((1, ML_STEP, n), lambda bi, i: (bi, i, 0)),
            pl.BlockSpec((1, ML_STEP, LANES), lambda bi, i: (bi, i, 0)),
            pl.BlockSpec((1, LANES), lambda bi, i: (0, 0)),
            pl.BlockSpec((LANES, ML_HEADS * LANES), lambda bi, i: (0, 0)),
        ],
        out_specs=pl.BlockSpec((1, ML_STEP, D_MODEL), lambda bi, i: (bi, i, 0)),
        out_shape=jax.ShapeDtypeStruct((b, s, D_MODEL), BF16),
        scratch_shapes=[
            pltpu.VMEM((ML_HEADS, LANES, 2 * LANES), F32),
            pltpu.VMEM((1, LANES), F32),
        ],
        compiler_params=pltpu.CompilerParams(dimension_semantics=("arbitrary", "arbitrary")),
        name="mlstm_core",
    )(main, gates, bias_row, _mlstm_expand())


def kernel(x, ln_mix_pre, ln_mix_post, ln_ffn_pre, ln_ffn_post, sb_w_qkv, sb_w_o, gc_w_in, gc_conv_w,
           gc_conv_b, gc_w_out, ml_w_in, ml_b_i, ml_b_f, ml_w_out, ffn_w_gu, ffn_w_down):
    bsz, seq, d = x.shape
    x2d = x.reshape(bsz * seq, d)
    w_in_src = ((sb_w_qkv, 3 * d), (gc_w_in, 3 * d), (ml_w_in, ML_MAIN_WIDTH))
    w_out_src = (sb_w_o, gc_w_out, ml_w_out)
    ml_gate = jnp.pad(ml_w_in[:, :, ML_MAIN_WIDTH:],
                      ((0, 0), (0, 0), (0, LANES - 2 * ML_HEADS))).astype(BF16)
    w_in = sb_w_qkv[:1].astype(BF16)
    for i in range(DEPTH):
        kind, j = i % N_MIXERS, i // N_MIXERS
        projs = [(w_in, 0, BF16)] + ([(ml_gate, j, F32)] if kind == 2 else [])
        cast = [(w_out_src[kind], j), (ffn_w_gu, i), (ffn_w_down, i)]
        if i + 1 < DEPTH:
            nxt, cols = w_in_src[(i + 1) % N_MIXERS]
            cast.append((nxt, (i + 1) // N_MIXERS, cols))
        proj, *rest = _norm_proj(x2d, ln_mix_pre[i], projs, cast=cast)
        wo, wgu, wd = rest[len(projs) - 1:len(projs) + 2]
        if i + 1 < DEPTH:
            w_in = rest[-1][None]
        conv = None
        if kind == 0:
            mixed = _sb_attention(proj.reshape(bsz, seq, 3 * d))
        elif kind == 1:
            mixed = proj
            conv = (gc_conv_w[j], gc_conv_b[j], seq)
        else:
            bias_row = jnp.pad(jnp.concatenate([ml_b_i[j], ml_b_f[j]]), (0, LANES - 2 * ML_HEADS))
            mixed = _mlstm_core(proj.reshape(bsz, seq, ML_MAIN_WIDTH), rest[0].reshape(bsz, seq, LANES),
                                bias_row.reshape(1, LANES))
        x2d = _out_ffn(mixed.reshape(bsz * seq, -1), x2d, wo, ln_mix_post[i], ln_ffn_pre[i],
                       wgu, wd, ln_ffn_post[i], conv)
    return x2d.reshape(bsz, seq, d)
```
